```python
import jax, jax.numpy as jnp
from jax import lax
import numpy as np

D_MODEL = 2048
BATCH = 4
SEQ = 8192
DEPTH = 4

HEAD_DIM = 128
D_MIX = D_MODEL
D_FF = 256 * ((8 * D_MODEL // 3 + 255) // 256)
EPS = 1e-6

SG_WIDTH = D_MIX // 4
SG_GROUPS = SG_WIDTH // HEAD_DIM
SG_CHUNK = 128

ATT_WIDTH = 3 * D_MIX // 8
ATT_HEADS = ATT_WIDTH // HEAD_DIM
DILATED_PAIRS = ((128, 1), (512, 4), (2048, 16))
ROPE_THETA = 500000.0
ROPE_DIMS = HEAD_DIM // 4

GLA_WIDTH = D_MIX - SG_WIDTH - ATT_WIDTH
GLA_HEADS = GLA_WIDTH // HEAD_DIM
GLA_DV = HEAD_DIM
GLA_DK = GLA_DV // 2
GLA_CHUNK = 64
GLA_GATE_RANK = 16
GLA_GATE_NORMALIZER = 16.0

SPLIT_SIZES = (SG_WIDTH, SG_WIDTH,
               ATT_WIDTH, ATT_WIDTH, ATT_WIDTH,
               GLA_HEADS * GLA_DK, GLA_HEADS * GLA_DK,
               GLA_WIDTH, GLA_WIDTH,
               GLA_GATE_RANK)
N_IN = sum(SPLIT_SIZES)

kernel_name = "hymba_gmlp_dilated_gla_macaron"


def rms_norm(x, g):
    xf = x.astype(jnp.float32)
    y = xf * lax.rsqrt(jnp.mean(xf * xf, axis=-1, keepdims=True) + EPS)
    return (y * g).astype(x.dtype)


def swiglu(h, w_gate, w_up, w_down):
    return (jax.nn.silu(h @ w_gate) * (h @ w_up)) @ w_down


def rope_tables(seq_len):
    pos = jnp.arange(seq_len, dtype=jnp.float32)
    inv_freq = ROPE_THETA ** (-jnp.arange(0, ROPE_DIMS, 2, dtype=jnp.float32) / ROPE_DIMS)
    ang = pos[:, None] * inv_freq[None, :]
    return jnp.cos(ang), jnp.sin(ang)


def apply_partial_rope(t, cos, sin):
    r = ROPE_DIMS // 2
    c = cos[None, :, None, :]
    s = sin[None, :, None, :]
    t1, t2, rest = t[..., :r], t[..., r:2 * r], t[..., 2 * r:]
    return jnp.concatenate([t1 * c - t2 * s, t2 * c + t1 * s, rest], axis=-1)


def spatial_gating(u, v, norm_g, w_s, b_s):
    B, S, _ = u.shape
    u = jax.nn.gelu(u)
    v = jax.nn.gelu(v)
    v = rms_norm(v.reshape(B, S, SG_GROUPS, HEAD_DIM), norm_g.reshape(SG_GROUPS, HEAD_DIM))
    v = v.reshape(B, S // SG_CHUNK, SG_CHUNK, SG_GROUPS, HEAD_DIM)
    causal = jnp.tril(jnp.ones((SG_CHUNK, SG_CHUNK), dtype=bool))
    w = jnp.where(causal[None], w_s, 0.0)
    sv = jnp.einsum('gts,bnsgc->bntgc', w, v) + b_s.T[:, :, None]
    return u * sv.reshape(B, S, SG_WIDTH)


def dilated_branch(q, k, v, window, dil):
    B, S, H, D = q.shape
    W = window // dil
    span = dil * W
    L = -(-S // span) * span
    nb = L // span

    def to_blocks(t):
        t = jnp.pad(t, ((0, 0), (0, L - S), (0, 0), (0, 0)))
        t = t.reshape(B, L // dil, dil, H, D).transpose(0, 2, 1, 3, 4)
        return t.reshape(B, dil, nb, W, H, D)

    def with_prev(t):
        prev = jnp.pad(t, ((0, 0), (0, 0), (1, 0), (0, 0), (0, 0), (0, 0)))[:, :, :-1]
        return jnp.concatenate([prev, t], axis=3)

    qb = to_blocks(q)
    kk = with_prev(to_blocks(k))
    vv = with_prev(to_blocks(v))
    s = jnp.einsum('brnqhd,brnkhd->brnhqk', qb, kk, preferred_element_type=jnp.float32)
    qi = jnp.arange(W)[:, None]
    kj = jnp.arange(2 * W)[None, :]
    dist = qi + W - kj
    band = (dist >= 0) & (dist <= W)
    blk = jnp.arange(nb)[:, None, None]
    valid = band[None] & ((blk > 0) | (kj[None] >= W))
    s = jnp.where(valid[None, None, :, None], s, -jnp.inf)
    m = jnp.max(s, axis=-1, keepdims=True)
    p = jnp.exp(s - m)
    l = jnp.sum(p, axis=-1, keepdims=True)
    o = jnp.einsum('brnhqk,brnkhd->brnqhd', p / l, vv)
    lse = (m + jnp.log(l))[..., 0]
    o = o.reshape(B, dil, L // dil, H, D).transpose(0, 2, 1, 3, 4).reshape(B, L, H, D)[:, :S]
    lse = lse.transpose(0, 1, 2, 4, 3).reshape(B, dil, L // dil, H)
    lse = lse.transpose(0, 2, 1, 3).reshape(B, L, H)[:, :S]
    return o, lse


def dilated_attention(q, k, v, q_gain, k_gain, cos, sin):
    B, S, _ = q.shape
    shp = (B, S, ATT_HEADS, HEAD_DIM)
    q = apply_partial_rope(rms_norm(q.reshape(shp).astype(jnp.float32), q_gain), cos, sin) * HEAD_DIM ** -0.5
    k = apply_partial_rope(rms_norm(k.reshape(shp).astype(jnp.float32), k_gain), cos, sin)
    v = v.reshape(shp).astype(jnp.float32)
    outs, lses = [], []
    for window, dil in DILATED_PAIRS:
        o, lse = dilated_branch(q, k, v, window, dil)
        outs.append(o)
        lses.append(lse)
    wts = jax.nn.softmax(jnp.stack(lses, axis=0), axis=0)
    o = jnp.einsum('gbsh,gbshd->bshd', wts, jnp.stack(outs, axis=0))
    return o.reshape(B, S, ATT_WIDTH)


def gated_linear_attention(q, k, v, g, r, w_gate, b_gate, out_g):
    B, S, _ = q.shape
    H, dk, dv, C = GLA_HEADS, GLA_DK, GLA_DV, GLA_CHUNK
    nc = S // C
    f32 = jnp.float32
    q = q.astype(f32).reshape(B, nc, C, H, dk) * dk ** -0.5
    k = k.astype(f32).reshape(B, nc, C, H, dk)
    v = v.astype(f32).reshape(B, nc, C, H, dv)
    log_a = jax.nn.log_sigmoid((r @ w_gate + b_gate).astype(f32)) / GLA_GATE_NORMALIZER
    b = jnp.cumsum(log_a.reshape(B, nc, C, H, dk), axis=2)
    q_dec = q * jnp.exp(b)
    A = jnp.einsum('bnihk,bnjhk->bnhij', q_dec, k * jnp.exp(-b))
    causal = jnp.tril(jnp.ones((C, C), dtype=bool))
    A = jnp.where(causal, A, 0.0)
    o = jnp.einsum('bnhij,bnjhv->bnihv', A, v)
    b_last = b[:, :, -1]
    U = jnp.einsum('bnjhk,bnjhv->nbhkv', k * jnp.exp(b_last[:, :, None] - b), v)
    decay = jnp.exp(b_last).transpose(1, 0, 2, 3)

    def step(state, inp):
        dec, u = inp
        return dec[..., None] * state + u, state

    _, states = lax.scan(step, jnp.zeros((B, H, dk, dv), f32), (decay, U))
    o = o + jnp.einsum('bnihk,nbhkv->bnihv', q_dec, states)
    o = rms_norm(o.reshape(B, S, H, dv), out_g) * jax.nn.silu(g.astype(f32).reshape(B, S, H, dv))
    return o.reshape(B, S, GLA_WIDTH)


def hybrid_mixer(h, w_in, sg_norm, sg_w, sg_b, q_norm, k_norm,
                 gla_w_gate, gla_b_gate, gla_out_norm, w_out, cos, sin):
    z = h @ w_in
    (a_u, a_v, b_q, b_k, b_v, c_q, c_k, c_v, c_g, c_r) = jnp.split(
        z, np.cumsum(SPLIT_SIZES)[:-1], axis=-1)
    ya = spatial_gating(a_u, a_v, sg_norm, sg_w, sg_b)
    yb = dilated_attention(b_q, b_k, b_v, q_norm, k_norm, cos, sin)
    yc = gated_linear_attention(c_q, c_k, c_v, c_g, c_r, gla_w_gate, gla_b_gate, gla_out_norm)
    y = jnp.concatenate([ya, yb.astype(h.dtype), yc.astype(h.dtype)], axis=-1)
    return y @ w_out


def setup_inputs(seed: int = 0) -> dict:
    key = jax.random.key(seed)
    ks = jax.random.split(key, 17)
    f32 = jnp.float32

    def nrm(k, shape, scale):
        return jax.random.normal(k, shape, f32) * scale

    def gain(k, shape):
        return 1.0 + 0.02 * jax.random.normal(k, shape, f32)

    return {
        "x": nrm(ks[0], (BATCH, SEQ, D_MODEL), 1.0),
        "ffn_norm": gain(ks[1], (DEPTH, 2, D_MODEL)),
        "ffn_w_gate": nrm(ks[2], (DEPTH, 2, D_MODEL, D_FF), D_MODEL ** -0.5),
        "ffn_w_up": nrm(ks[3], (DEPTH, 2, D_MODEL, D_FF), D_MODEL ** -0.5),
        "ffn_w_down": nrm(ks[4], (DEPTH, 2, D_FF, D_MODEL), D_FF ** -0.5),
        "mix_norm": gain(ks[5], (DEPTH, D_MODEL)),
        "w_in": nrm(ks[6], (DEPTH, D_MODEL, N_IN), D_MODEL ** -0.5),
        "sg_norm": gain(ks[7], (DEPTH, SG_WIDTH)),
        "sg_w": nrm(ks[8], (DEPTH, SG_GROUPS, SG_CHUNK, SG_CHUNK), SG_CHUNK ** -0.5),
        "sg_b": gain(ks[9], (DEPTH, SG_GROUPS, SG_CHUNK)),
        "q_norm": gain(ks[10], (DEPTH, HEAD_DIM)),
        "k_norm": gain(ks[11], (DEPTH, HEAD_DIM)),
        "gla_w_gate": nrm(ks[12], (DEPTH, GLA_GATE_RANK, GLA_HEADS * GLA_DK), GLA_GATE_RANK ** -0.5),
        "gla_b_gate": nrm(ks[13], (DEPTH, GLA_HEADS * GLA_DK), 0.1),
        "gla_out_norm": gain(ks[14], (DEPTH, GLA_DV)),
        "w_out": nrm(ks[15], (DEPTH, D_MIX, D_MODEL), D_MIX ** -0.5),
    }


def reference(x, ffn_norm, ffn_w_gate, ffn_w_up, ffn_w_down, mix_norm, w_in,
              sg_norm, sg_w, sg_b, q_norm, k_norm, gla_w_gate, gla_b_gate,
              gla_out_norm, w_out):
    cos, sin = rope_tables(x.shape[1])
    for l in range(DEPTH):
        x = x + 0.5 * swiglu(rms_norm(x, ffn_norm[l, 0]), ffn_w_gate[l, 0], ffn_w_up[l, 0], ffn_w_down[l, 0])
        x = x + hybrid_mixer(rms_norm(x, mix_norm[l]), w_in[l], sg_norm[l], sg_w[l], sg_b[l],
                             q_norm[l], k_norm[l], gla_w_gate[l], gla_b_gate[l],
                             gla_out_norm[l], w_out[l], cos, sin)
        x = x + 0.5 * swiglu(rms_norm(x, ffn_norm[l, 1]), ffn_w_gate[l, 1], ffn_w_up[l, 1], ffn_w_down[l, 1])
    return x
```

```python
import functools

import jax
import jax.numpy as jnp
from jax import lax
from jax.experimental import pallas as pl
from jax.experimental.pallas import tpu as pltpu

F32 = jnp.float32
BF16 = jnp.bfloat16

EPS = 1e-6
HEAD_DIM = 128
LANES = 128
SG_CHUNK = 128
ATT_DILATIONS = (1, 4, 16)
ATT_STEPS = 128
ROPE_THETA = 500000.0
ROPE_DIMS = HEAD_DIM // 4
GLA_DK = 64
GLA_CHUNK = 64
GLA_GATE_RANK = 16
GLA_GATE_NORMALIZER = 16.0
MASK_VALUE = -1e30

VMEM_LIMIT_BYTES = 56 * 1024 * 1024

FFN_TM, FFN_TF = 512, 512
PROJ_TM = 256
SG_ROWS = 1024
GLA_ROWS = 1024
ATT_PREP_ROWS = 256


def _params(*semantics):
    return pltpu.CompilerParams(dimension_semantics=semantics, vmem_limit_bytes=VMEM_LIMIT_BYTES)


def _rms_norm(x, g):
    return x * lax.rsqrt(jnp.mean(x * x, axis=-1, keepdims=True) + EPS) * g


def _dot(a, b):
    return jnp.dot(a, b, preferred_element_type=F32)


def _dot_nt(a, b):
    return lax.dot_general(a, b, (((1,), (1,)), ((), ())), preferred_element_type=F32)


def _dot_tn(a, b):
    return lax.dot_general(a, b, (((0,), (0,)), ((), ())), preferred_element_type=F32)


def _ffn_kernel(x_ref, g_ref, wg_ref, wu_ref, wd_ref, o_ref, h_ref):
    @pl.when(pl.program_id(1) == 0)
    def _():
        x = x_ref[...]
        h_ref[...] = _rms_norm(x, g_ref[...]).astype(BF16)
        o_ref[...] = x

    h = h_ref[...]
    gate = _dot(h, wg_ref[...])
    up = _dot(h, wu_ref[...])
    act = (gate * jax.nn.sigmoid(gate)) * (0.5 * up)
    o_ref[...] += _dot(act.astype(BF16), wd_ref[...])


def _ffn(x, norm_g, w_gate, w_up, w_down, l, j):
    n, d = x.shape
    f = w_gate.shape[-1]
    tm, tf = min(FFN_TM, n), FFN_TF
    return pl.pallas_call(
        _ffn_kernel,
        grid=(n // tm, f // tf),
        in_specs=[
            pl.BlockSpec((tm, d), lambda i, k: (i, 0)),
            pl.BlockSpec((None, None, 1, d), lambda i, k: (l, j, 0, 0)),
            pl.BlockSpec((None, None, d, tf), lambda i, k: (l, j, 0, k)),
            pl.BlockSpec((None, None, d, tf), lambda i, k: (l, j, 0, k)),
            pl.BlockSpec((None, None, tf, d), lambda i, k: (l, j, k, 0)),
        ],
        out_specs=pl.BlockSpec((tm, d), lambda i, k: (i, 0)),
        out_shape=jax.ShapeDtypeStruct((n, d), F32),
        scratch_shapes=[pltpu.VMEM((tm, d), BF16)],
        compiler_params=_params("parallel", "arbitrary"),
        name="ffn",
    )(x, norm_g, w_gate, w_up, w_down)


def _inproj_kernel(x_ref, g_ref, w_ref, zs_ref, za_ref, zg_ref):
    h = _rms_norm(x_ref[...], g_ref[...]).astype(BF16)
    c = 0
    for z_ref in (zs_ref, za_ref, zg_ref):
        wdt = z_ref.shape[1]
        z_ref[...] = _dot(h, w_ref[:, c:c + wdt]).astype(BF16)
        c += wdt


def _inproj(x, norm_g, w_in, l, widths):
    n, d = x.shape
    c = w_in.shape[-1]
    tm = min(PROJ_TM, n)
    return pl.pallas_call(
        _inproj_kernel,
        grid=(n // tm,),
        in_specs=[
            pl.BlockSpec((tm, d), lambda i: (i, 0)),
            pl.BlockSpec((None, 1, d), lambda i: (l, 0, 0)),
            pl.BlockSpec((None, d, c), lambda i: (l, 0, 0), pipeline_mode=pl.Buffered(1)),
        ],
        out_specs=[pl.BlockSpec((tm, w), lambda i: (i, 0)) for w in widths],
        out_shape=[jax.ShapeDtypeStruct((n, w), BF16) for w in widths],
        compiler_params=_params("parallel"),
        name="in_proj",
    )(x, norm_g, w_in)


def _sgu_kernel(z_ref, ng_ref, w_ref, bt_ref, o_ref):
    rows, width = o_ref.shape
    groups = width // HEAD_DIM
    t = SG_CHUNK
    r_i = lax.broadcasted_iota(jnp.int32, (t, t), 0)
    c_i = lax.broadcasted_iota(jnp.int32, (t, t), 1)
    causal = c_i <= r_i
    for g in range(groups):
        cu = slice(g * HEAD_DIM, (g + 1) * HEAD_DIM)
        cv = slice(width + g * HEAD_DIM, width + (g + 1) * HEAD_DIM)
        w = jnp.where(causal, w_ref[g], 0.0).astype(BF16)
        b_col = bt_ref[:, g:g + 1]
        gain = ng_ref[:, cu]
        for c in range(rows // t):
            rs = slice(c * t, (c + 1) * t)
            v = _rms_norm(jax.nn.gelu(z_ref[rs, cv].astype(F32)), gain)
            sv = _dot(w, v.astype(BF16)) + b_col
            u = jax.nn.gelu(z_ref[rs, cu].astype(F32))
            o_ref[rs, cu] = (u * sv).astype(BF16)


def _sgu(zs, sg_norm, sg_w, sg_bt, l):
    n, w2 = zs.shape
    width = w2 // 2
    groups = width // HEAD_DIM
    rows = min(SG_ROWS, n)
    return pl.pallas_call(
        _sgu_kernel,
        grid=(n // rows,),
        in_specs=[
            pl.BlockSpec((rows, w2), lambda i: (i, 0)),
            pl.BlockSpec((None, 1, width), lambda i: (l, 0, 0)),
            pl.BlockSpec((None, groups, SG_CHUNK, SG_CHUNK), lambda i: (l, 0, 0, 0)),
            pl.BlockSpec((None, SG_CHUNK, groups), lambda i: (l, 0, 0)),
        ],
        out_specs=pl.BlockSpec((rows, width), lambda i: (i, 0)),
        out_shape=jax.ShapeDtypeStruct((n, width), BF16),
        compiler_params=_params("parallel"),
        name="spatial_gating",
    )(zs, sg_norm, sg_w, sg_bt)


def _att_kernel(q_ref, k_ref, v_ref, cos_ref, sin_ref, qg_ref, kg_ref, o_ref,
                qs, ks, vs, acc, m_s, l_s):
    seq = q_ref.shape[0]
    w = ATT_STEPS
    lane = lax.broadcasted_iota(jnp.int32, (1, HEAD_DIM), 1)
    half = ROPE_DIMS // 2

    def rope(t, cos, sin):
        partner = jnp.where(lane < half, pltpu.roll(t, HEAD_DIM - half, 1), pltpu.roll(t, half, 1))
        return t * cos + partner * sin

    pr = min(ATT_PREP_ROWS, seq)

    def prep(c, carry):
        rows = pl.ds(pl.multiple_of(c * pr, pr), pr)
        cos = cos_ref[rows, :]
        sin = sin_ref[rows, :]
        q = rope(_rms_norm(q_ref[rows, :].astype(F32), qg_ref[...]), cos, sin) * HEAD_DIM ** -0.5
        qs[rows, :] = q
        ks[rows, :] = rope(_rms_norm(k_ref[rows, :].astype(F32), kg_ref[...]), cos, sin)
        vs[rows, :] = v_ref[rows, :].astype(F32)
        return carry

    lax.fori_loop(0, seq // pr, prep, 0)

    r_i = lax.broadcasted_iota(jnp.int32, (w, 2 * w), 0)
    c_i = lax.broadcasted_iota(jnp.int32, (w, 2 * w), 1)
    band = (c_i >= r_i) & (c_i - r_i <= w)

    for bi, d in enumerate(ATT_DILATIONS):
        first = bi == 0
        last = bi == len(ATT_DILATIONS) - 1
        span = w * d
        nb = seq // span

        def block(i, carry, d=d, span=span, nb=nb, first=first, last=last):
            r = i // nb
            n = i % nb
            start = r + n * span
            pstart = jnp.where(n > 0, start - span, start)
            if d == 1:
                cur = pl.ds(pl.multiple_of(start, w), w)
                prev = pl.ds(pl.multiple_of(pstart, w), w)
            else:
                cur = pl.ds(start, w, stride=d)
                prev = pl.ds(pstart, w, stride=d)
            q = qs[cur, :].astype(BF16)
            k2 = jnp.concatenate([ks[prev, :], ks[cur, :]], axis=0).astype(BF16)
            s = _dot_nt(q, k2)
            first_valid = jnp.where(n > 0, 0, w)
            s = jnp.where(band & (c_i >= first_valid), s, MASK_VALUE)
            m_blk = jnp.max(s, axis=1, keepdims=True)
            if first:
                m_new = jnp.broadcast_to(m_blk, (w, HEAD_DIM))
            else:
                m_old = m_s[cur, :]
                m_new = jnp.maximum(m_old, m_blk)
            p = jnp.exp(s - jnp.concatenate([m_new, m_new], axis=1))
            p_sum = jnp.sum(p, axis=1, keepdims=True)
            v2 = jnp.concatenate([vs[prev, :], vs[cur, :]], axis=0).astype(BF16)
            pv = _dot(p.astype(BF16), v2)
            if first:
                l_new = jnp.broadcast_to(p_sum, (w, HEAD_DIM))
                acc_new = pv
            else:
                alpha = jnp.exp(m_old - m_new)
                l_new = alpha * l_s[cur, :] + p_sum
                acc_new = alpha * acc[cur, :] + pv
            if last:
                acc[cur, :] = acc_new / l_new
            else:
                m_s[cur, :] = m_new
                l_s[cur, :] = l_new
                acc[cur, :] = acc_new
            return carry

        lax.fori_loop(0, seq // w, block, 0)

    def emit(c, carry):
        rows = pl.ds(pl.multiple_of(c * pr, pr), pr)
        o_ref[rows, :] = acc[rows, :].astype(BF16)
        return carry

    lax.fori_loop(0, seq // pr, emit, 0)


def _attention(za, cos_t, sin_t, q_gain, k_gain, l, batch):
    n, w3 = za.shape
    seq = n // batch
    width = w3 // 3
    heads = width // HEAD_DIM
    assert seq % (ATT_STEPS * max(ATT_DILATIONS)) == 0
    za3 = za.reshape(batch, seq, w3)
    blk = (None, seq, HEAD_DIM)
    const = functools.partial(pl.BlockSpec, pipeline_mode=pl.Buffered(1))
    out = pl.pallas_call(
        _att_kernel,
        grid=(batch, heads),
        in_specs=[
            pl.BlockSpec(blk, lambda b, h: (b, 0, h)),
            pl.BlockSpec(blk, lambda b, h: (b, 0, heads + h)),
            pl.BlockSpec(blk, lambda b, h: (b, 0, 2 * heads + h)),
            const((seq, HEAD_DIM), lambda b, h: (0, 0)),
            const((seq, HEAD_DIM), lambda b, h: (0, 0)),
            pl.BlockSpec((None, 1, HEAD_DIM), lambda b, h: (l, 0, 0)),
            pl.BlockSpec((None, 1, HEAD_DIM), lambda b, h: (l, 0, 0)),
        ],
        out_specs=pl.BlockSpec(blk, lambda b, h: (b, 0, h)),
        out_shape=jax.ShapeDtypeStruct((batch, seq, width), BF16),
        scratch_shapes=[pltpu.VMEM((seq, HEAD_DIM), F32) for _ in range(6)],
        compiler_params=_params("parallel", "parallel"),
        name="dilated_attention",
    )(za3, za3, za3, cos_t, sin_t, q_gain, k_gain)
    return out.reshape(n, width)


def _gla_kernel(q_ref, k_ref, v_ref, g_ref, r_ref, wg_ref, bg_ref, og_ref, o_ref, st_ref, o_acc):
    rows = q_ref.shape[0]
    ck = GLA_CHUNK
    dv = HEAD_DIM

    @pl.when(pl.program_id(2) == 0)
    def _():
        st_ref[...] = jnp.zeros_like(st_ref)

    x = _dot(r_ref[...], wg_ref[...]) + bg_ref[...]
    log_a = (jnp.minimum(x, 0.0) - jnp.log1p(jnp.exp(-jnp.abs(x)))) / GLA_GATE_NORMALIZER
    hi = log_a.astype(BF16)
    lo = (log_a - hi.astype(F32)).astype(BF16)
    grp = 4 * ck
    r_i = lax.broadcasted_iota(jnp.int32, (grp, grp), 0)
    c_i = lax.broadcasted_iota(jnp.int32, (grp, grp), 1)
    same = (r_i // ck) == (c_i // ck)
    tri = jnp.where(same & (c_i <= r_i), 1.0, 0.0).astype(BF16)
    ones = jnp.where(same, 1.0, 0.0).astype(BF16)
    b_parts, bl_parts = [], []
    for s in range(rows // grp):
        rs = slice(s * grp, (s + 1) * grp)
        b_parts.append(_dot(tri, hi[rs]) + _dot(tri, lo[rs]))
        bl_parts.append(_dot(ones, hi[rs]) + _dot(ones, lo[rs]))
    b = jnp.concatenate(b_parts, axis=0)
    b_last = jnp.concatenate(bl_parts, axis=0)

    q = q_ref[...].astype(F32) * GLA_DK ** -0.5
    k = k_ref[...].astype(F32)
    q_dec = (q * jnp.exp(b)).astype(BF16)
    k_inv = (k * jnp.exp(-b)).astype(BF16)
    k_dec = (k * jnp.exp(b_last - b)).astype(BF16)
    decay = jnp.exp(b_last)

    lane = lax.broadcasted_iota(jnp.int32, (1, 2 * GLA_DK), 1)
    head0 = lane < GLA_DK
    ar = lax.broadcasted_iota(jnp.int32, (ck, 2 * ck), 0)
    ac = lax.broadcasted_iota(jnp.int32, (ck, 2 * ck), 1)
    causal = (ac % ck) <= ar
    sr = lax.broadcasted_iota(jnp.int32, (2 * dv, 2 * GLA_DK), 0)
    sc = lax.broadcasted_iota(jnp.int32, (2 * dv, 2 * GLA_DK), 1)
    own = (sr < dv) == (sc < GLA_DK)
    zeros_v = jnp.zeros((ck, dv), BF16)

    state = st_ref[...]
    for c in range(rows // ck):
        rs = slice(c * ck, (c + 1) * ck)
        qd = q_dec[rs]
        ki = k_inv[rs]
        ki2 = jnp.concatenate([jnp.where(head0, ki, 0), jnp.where(head0, 0, ki)], axis=0)
        a = jnp.where(causal, _dot_nt(qd, ki2), 0.0)
        vp = v_ref[rs, :]
        v_bd = jnp.concatenate(
            [jnp.concatenate([vp[:, :dv], zeros_v], axis=1),
             jnp.concatenate([zeros_v, vp[:, dv:]], axis=1)], axis=0)
        o_acc[rs, :] = _dot(a.astype(BF16), v_bd) + _dot_nt(qd, state.astype(BF16))
        u = _dot_tn(vp, k_dec[rs])
        state = state * decay[c * ck:c * ck + 1, :] + jnp.where(own, u, 0.0)
    st_ref[...] = state

    for h in range(2):
        cs = slice(h * dv, (h + 1) * dv)
        g = g_ref[:, cs].astype(F32)
        o = _rms_norm(o_acc[:, cs], og_ref[...]) * (g * jax.nn.sigmoid(g))
        o_ref[:, cs] = o.astype(BF16)


def _gla(zg, w_gate, b_gate, out_gain, l, batch):
    n, _ = zg.shape
    seq = n // batch
    dkp = 2 * GLA_DK
    dvp = 2 * HEAD_DIM
    pairs = w_gate.shape[-1] // dkp
    width = pairs * dvp
    rows = min(GLA_ROWS, seq)
    zg3 = zg.reshape(batch, seq, zg.shape[1])
    qk_blocks = 2 * pairs * dkp // dvp
    out = pl.pallas_call(
        _gla_kernel,
        grid=(batch, pairs, seq // rows),
        in_specs=[
            pl.BlockSpec((None, rows, dkp), lambda b, p, t: (b, t, p)),
            pl.BlockSpec((None, rows, dkp), lambda b, p, t: (b, t, pairs + p)),
            pl.BlockSpec((None, rows, dvp), lambda b, p, t: (b, t, qk_blocks + p)),
            pl.BlockSpec((None, rows, dvp), lambda b, p, t: (b, t, qk_blocks + pairs + p)),
            pl.BlockSpec((None, rows, LANES), lambda b, p, t: (b, t, (2 * pairs * dkp + 2 * width) // LANES)),
            pl.BlockSpec((None, LANES, dkp), lambda b, p, t: (l, 0, p)),
            pl.BlockSpec((None, 1, dkp), lambda b, p, t: (l, 0, p)),
            pl.BlockSpec((None, 1, HEAD_DIM), lambda b, p, t: (l, 0, 0)),
        ],
        out_specs=pl.BlockSpec((None, rows, dvp), lambda b, p, t: (b, t, p)),
        out_shape=jax.ShapeDtypeStruct((batch, seq, width), BF16),
        scratch_shapes=[pltpu.VMEM((dvp, dkp), F32), pltpu.VMEM((rows, dvp), F32)],
        compiler_params=_params("parallel", "parallel", "arbitrary"),
        name="gated_linear_attention",
    )(zg3, zg3, zg3, zg3, zg3, w_gate, b_gate, out_gain)
    return out.reshape(n, width)


def _outproj_kernel(x_ref, ya_ref, yb_ref, yc_ref, w_ref, o_ref):
    acc = x_ref[...]
    c = 0
    for y_ref in (ya_ref, yb_ref, yc_ref):
        wdt = y_ref.shape[1]
        acc = acc + _dot(y_ref[...], w_ref[c:c + wdt, :])
        c += wdt
    o_ref[...] = acc


def _outproj(x, ys, w_out, l):
    n, d = x.shape
    tm = min(PROJ_TM, n)
    return pl.pallas_call(
        _outproj_kernel,
        grid=(n // tm,),
        in_specs=[pl.BlockSpec((tm, d), lambda i: (i, 0))]
        + [pl.BlockSpec((tm, y.shape[1]), lambda i: (i, 0)) for y in ys]
        + [pl.BlockSpec((None, w_out.shape[1], d), lambda i: (l, 0, 0), pipeline_mode=pl.Buffered(1))],
        out_specs=pl.BlockSpec((tm, d), lambda i: (i, 0)),
        out_shape=jax.ShapeDtypeStruct((n, d), F32),
        compiler_params=_params("parallel"),
        name="out_proj",
    )(x, *ys, w_out)


def _rope_tables(seq):
    pos = jnp.arange(seq, dtype=F32)
    inv_freq = ROPE_THETA ** (-jnp.arange(0, ROPE_DIMS, 2, dtype=F32) / ROPE_DIMS)
    ang = pos[:, None] * inv_freq[None, :]
    cos, sin = jnp.cos(ang), jnp.sin(ang)
    rest = HEAD_DIM - ROPE_DIMS
    cos_t = jnp.concatenate([cos, cos, jnp.ones((seq, rest), F32)], axis=1)
    sin_t = jnp.concatenate([-sin, sin, jnp.zeros((seq, rest), F32)], axis=1)
    return cos_t, sin_t


def kernel(x, ffn_norm, ffn_w_gate, ffn_w_up, ffn_w_down, mix_norm, w_in, sg_norm, sg_w, sg_b, q_norm, k_norm, gla_w_gate, gla_b_gate, gla_out_norm, w_out):
    batch, seq, d = x.shape
    depth = w_in.shape[0]
    n = batch * seq
    sg_width = sg_norm.shape[-1]
    gla_qk = gla_w_gate.shape[-1]
    gla_width = (gla_qk // GLA_DK) * HEAD_DIM
    att_width = d - sg_width - gla_width
    n_in = w_in.shape[-1]
    assert n_in == 2 * sg_width + 3 * att_width + 2 * gla_qk + 2 * gla_width + GLA_GATE_RANK
    widths = (2 * sg_width, 3 * att_width, 2 * gla_qk + 2 * gla_width + LANES)

    wg, wu, wd = ffn_w_gate.astype(BF16), ffn_w_up.astype(BF16), ffn_w_down.astype(BF16)
    w_in_p = jnp.pad(w_in, ((0, 0), (0, 0), (0, LANES - GLA_GATE_RANK))).astype(BF16)
    w_out_b = w_out.astype(BF16)
    gla_wg = jnp.pad(gla_w_gate, ((0, 0), (0, LANES - GLA_GATE_RANK), (0, 0))).astype(BF16)
    ffn_g = ffn_norm.reshape(depth, 2, 1, d)
    mix_g = mix_norm.reshape(depth, 1, d)
    sg_g = sg_norm.reshape(depth, 1, sg_width)
    sg_bt = jnp.swapaxes(sg_b, 1, 2)
    q_g = q_norm.reshape(depth, 1, HEAD_DIM)
    k_g = k_norm.reshape(depth, 1, HEAD_DIM)
    gla_bg = gla_b_gate.reshape(depth, 1, gla_qk)
    gla_og = gla_out_norm.reshape(depth, 1, HEAD_DIM)
    cos_t, sin_t = _rope_tables(seq)

    h = x.reshape(n, d)
    for l in range(depth):
        h = _ffn(h, ffn_g, wg, wu, wd, l, 0)
        zs, za, zg = _inproj(h, mix_g, w_in_p, l, widths)
        ya = _sgu(zs, sg_g, sg_w, sg_bt, l)
        yb = _attention(za, cos_t, sin_t, q_g, k_g, l, batch)
        yc = _gla(zg, gla_wg, gla_bg, gla_og, l, batch)
        h = _outproj(h, (ya, yb, yc), w_out_b, l)
        h = _ffn(h, ffn_g, wg, wu, wd, l, 1)
    return h.reshape(batch, seq, d)
```

```python
import functools

import jax
import jax.numpy as jnp
from jax import lax
from jax.experimental import pallas as pl
from jax.experimental.pallas import tpu as pltpu

F32 = jnp.float32
BF16 = jnp.bfloat16

EPS = 1e-6
HEAD_DIM = 128
LANES = 128
SG_CHUNK = 128
ATT_DILATIONS = (1, 4, 16)
ATT_STEPS = 128
ROPE_THETA = 500000.0
ROPE_DIMS = HEAD_DIM // 4
GLA_DK = 64
GLA_CHUNK = 64
GLA_GATE_RANK = 16
GLA_GATE_NORMALIZER = 16.0
MASK_VALUE = -1e30

VMEM_LIMIT_BYTES = 56 * 1024 * 1024

FFN_TM, FFN_TF = 1024, 512
PROJ_TM = 512
SG_ROWS = 1024
GLA_ROWS = 1024
ATT_PREP_ROWS = 256
ATT_UNROLL = 4


def _params(*semantics):
    return pltpu.CompilerParams(dimension_semantics=semantics, vmem_limit_bytes=VMEM_LIMIT_BYTES)


def _rms_norm(x, g):
    return x * lax.rsqrt(jnp.mean(x * x, axis=-1, keepdims=True) + EPS) * g


def _dot(a, b):
    return jnp.dot(a, b, preferred_element_type=F32)


def _dot_nt(a, b):
    return lax.dot_general(a, b, (((1,), (1,)), ((), ())), preferred_element_type=F32)


def _dot_tn(a, b):
    return lax.dot_general(a, b, (((0,), (0,)), ((), ())), preferred_element_type=F32)


def _ffn_kernel(x_ref, g_ref, wg_ref, wu_ref, wd_ref, o_ref, h_ref):
    @pl.when(pl.program_id(1) == 0)
    def _():
        x = x_ref[...]
        h_ref[...] = _rms_norm(x, g_ref[...]).astype(BF16)
        o_ref[...] = x

    h = h_ref[...]
    gate = _dot(h, wg_ref[...])
    up = _dot(h, wu_ref[...])
    act = (gate * jax.nn.sigmoid(gate)) * (0.5 * up)
    o_ref[...] += _dot(act.astype(BF16), wd_ref[...])


def _ffn(x, norm_g, w_gate, w_up, w_down, l, j):
    n, d = x.shape
    f = w_gate.shape[-1]
    tm, tf = min(FFN_TM, n), FFN_TF
    return pl.pallas_call(
        _ffn_kernel,
        grid=(n // tm, f // tf),
        in_specs=[
            pl.BlockSpec((tm, d), lambda i, k: (i, 0)),
            pl.BlockSpec((None, None, 1, d), lambda i, k: (l, j, 0, 0)),
            pl.BlockSpec((None, None, d, tf), lambda i, k: (l, j, 0, k)),
            pl.BlockSpec((None, None, d, tf), lambda i, k: (l, j, 0, k)),
            pl.BlockSpec((None, None, tf, d), lambda i, k: (l, j, k, 0)),
        ],
        out_specs=pl.BlockSpec((tm, d), lambda i, k: (i, 0)),
        out_shape=jax.ShapeDtypeStruct((n, d), F32),
        scratch_shapes=[pltpu.VMEM((tm, d), BF16)],
        compiler_params=_params("parallel", "arbitrary"),
        name="ffn",
    )(x, norm_g, w_gate, w_up, w_down)


def _inproj_kernel(x_ref, g_ref, w_ref, zs_ref, za_ref, zg_ref):
    h = _rms_norm(x_ref[...], g_ref[...]).astype(BF16)
    c = 0
    for z_ref in (zs_ref, za_ref, zg_ref):
        wdt = z_ref.shape[1]
        z_ref[...] = _dot(h, w_ref[:, c:c + wdt]).astype(BF16)
        c += wdt


def _inproj(x, norm_g, w_in, l, widths):
    n, d = x.shape
    c = w_in.shape[-1]
    tm = min(PROJ_TM, n)
    return pl.pallas_call(
        _inproj_kernel,
        grid=(n // tm,),
        in_specs=[
            pl.BlockSpec((tm, d), lambda i: (i, 0)),
            pl.BlockSpec((None, 1, d), lambda i: (l, 0, 0)),
            pl.BlockSpec((None, d, c), lambda i: (l, 0, 0), pipeline_mode=pl.Buffered(1)),
        ],
        out_specs=[pl.BlockSpec((tm, w), lambda i: (i, 0)) for w in widths],
        out_shape=[jax.ShapeDtypeStruct((n, w), BF16) for w in widths],
        compiler_params=_params("parallel"),
        name="in_proj",
    )(x, norm_g, w_in)


def _sgu_kernel(z_ref, ng_ref, w_ref, bt_ref, o_ref):
    rows, width = o_ref.shape
    groups = width // HEAD_DIM
    t = SG_CHUNK
    r_i = lax.broadcasted_iota(jnp.int32, (t, t), 0)
    c_i = lax.broadcasted_iota(jnp.int32, (t, t), 1)
    causal = c_i <= r_i
    for g in range(groups):
        cu = slice(g * HEAD_DIM, (g + 1) * HEAD_DIM)
        cv = slice(width + g * HEAD_DIM, width + (g + 1) * HEAD_DIM)
        w = jnp.where(causal, w_ref[g], 0.0).astype(BF16)
        b_col = bt_ref[:, g:g + 1]
        gain = ng_ref[:, cu]
        for c in range(rows // t):
            rs = slice(c * t, (c + 1) * t)
            v = _rms_norm(jax.nn.gelu(z_ref[rs, cv].astype(F32)), gain)
            sv = _dot(w, v.astype(BF16)) + b_col
            u = jax.nn.gelu(z_ref[rs, cu].astype(F32))
            o_ref[rs, cu] = (u * sv).astype(BF16)


def _sgu(zs, sg_norm, sg_w, sg_bt, l):
    n, w2 = zs.shape
    width = w2 // 2
    groups = width // HEAD_DIM
    rows = min(SG_ROWS, n)
    return pl.pallas_call(
        _sgu_kernel,
        grid=(n // rows,),
        in_specs=[
            pl.BlockSpec((rows, w2), lambda i: (i, 0)),
            pl.BlockSpec((None, 1, width), lambda i: (l, 0, 0)),
            pl.BlockSpec((None, groups, SG_CHUNK, SG_CHUNK), lambda i: (l, 0, 0, 0)),
            pl.BlockSpec((None, SG_CHUNK, groups), lambda i: (l, 0, 0)),
        ],
        out_specs=pl.BlockSpec((rows, width), lambda i: (i, 0)),
        out_shape=jax.ShapeDtypeStruct((n, width), BF16),
        compiler_params=_params("parallel"),
        name="spatial_gating",
    )(zs, sg_norm, sg_w, sg_bt)


def _att_kernel(q_ref, k_ref, v_ref, cos_ref, sin_ref, qg_ref, kg_ref, o_ref,
                qs, ks, vs, acc, m_s, l_s):
    seq = q_ref.shape[0]
    w = ATT_STEPS
    lane = lax.broadcasted_iota(jnp.int32, (1, HEAD_DIM), 1)
    half = ROPE_DIMS // 2

    def rope(t, cos, sin):
        partner = jnp.where(lane < half, pltpu.roll(t, HEAD_DIM - half, 1), pltpu.roll(t, half, 1))
        return t * cos + partner * sin

    pr = min(ATT_PREP_ROWS, seq)

    def prep(c, carry):
        rows = pl.ds(pl.multiple_of(c * pr, pr), pr)
        cos = cos_ref[rows, :]
        sin = sin_ref[rows, :]
        q = rope(_rms_norm(q_ref[rows, :].astype(F32), qg_ref[...]), cos, sin) * HEAD_DIM ** -0.5
        qs[rows, :] = q
        ks[rows, :] = rope(_rms_norm(k_ref[rows, :].astype(F32), kg_ref[...]), cos, sin)
        vs[rows, :] = v_ref[rows, :].astype(F32)
        return carry

    lax.fori_loop(0, seq // pr, prep, 0)

    diff = lax.broadcasted_iota(jnp.int32, (w, w), 1) - lax.broadcasted_iota(jnp.int32, (w, w), 0)
    cur_ok = diff <= 0

    for bi, d in enumerate(ATT_DILATIONS):
        first = bi == 0
        last = bi == len(ATT_DILATIONS) - 1
        span = w * d
        nb = seq // span

        def block(i, carry, d=d, span=span, nb=nb, first=first, last=last):
            r = i // nb
            n = i % nb
            start = r + n * span
            pstart = jnp.where(n > 0, start - span, start)
            if d == 1:
                cur = pl.ds(pl.multiple_of(start, w), w)
                prev = pl.ds(pl.multiple_of(pstart, w), w)
            else:
                cur = pl.ds(start, w, stride=d)
                prev = pl.ds(pstart, w, stride=d)
            q = qs[cur, :].astype(BF16)
            k2 = jnp.concatenate([ks[prev, :], ks[cur, :]], axis=0).astype(BF16)
            s = _dot_nt(q, k2)
            prev_min = jnp.where(n > 0, 0, 2 * w)
            s = jnp.concatenate([jnp.where(diff >= prev_min, s[:, :w], MASK_VALUE),
                                 jnp.where(cur_ok, s[:, w:], MASK_VALUE)], axis=1)
            m_blk = jnp.max(s, axis=1, keepdims=True)
            if first:
                m_new = jnp.broadcast_to(m_blk, (w, HEAD_DIM))
            else:
                m_old = m_s[cur, :]
                m_new = jnp.maximum(m_old, m_blk)
            p = jnp.exp(s - jnp.concatenate([m_new, m_new], axis=1))
            p_sum = jnp.sum(p, axis=1, keepdims=True)
            v2 = jnp.concatenate([vs[prev, :], vs[cur, :]], axis=0).astype(BF16)
            pv = _dot(p.astype(BF16), v2)
            if first:
                l_new = jnp.broadcast_to(p_sum, (w, HEAD_DIM))
                acc_new = pv
            else:
                alpha = jnp.exp(m_old - m_new)
                l_new = alpha * l_s[cur, :] + p_sum
                acc_new = alpha * acc[cur, :] + pv
            if last:
                acc[cur, :] = acc_new / l_new
            else:
                m_s[cur, :] = m_new
                l_s[cur, :] = l_new
                acc[cur, :] = acc_new
            return carry

        lax.fori_loop(0, seq // w, block, 0, unroll=ATT_UNROLL)

    def emit(c, carry):
        rows = pl.ds(pl.multiple_of(c * pr, pr), pr)
        o_ref[rows, :] = acc[rows, :].astype(BF16)
        return carry

    lax.fori_loop(0, seq // pr, emit, 0)


def _attention(za, cos_t, sin_t, q_gain, k_gain, l, batch):
    n, w3 = za.shape
    seq = n // batch
    width = w3 // 3
    heads = width // HEAD_DIM
    assert seq % (ATT_STEPS * max(ATT_DILATIONS)) == 0
    za3 = za.reshape(batch, seq, w3)
    blk = (None, seq, HEAD_DIM)
    const = functools.partial(pl.BlockSpec, pipeline_mode=pl.Buffered(1))
    out = pl.pallas_call(
        _att_kernel,
        grid=(batch, heads),
        in_specs=[
            pl.BlockSpec(blk, lambda b, h: (b, 0, h)),
            pl.BlockSpec(blk, lambda b, h: (b, 0, heads + h)),
            pl.BlockSpec(blk, lambda b, h: (b, 0, 2 * heads + h)),
            const((seq, HEAD_DIM), lambda b, h: (0, 0)),
            const((seq, HEAD_DIM), lambda b, h: (0, 0)),
            pl.BlockSpec((None, 1, HEAD_DIM), lambda b, h: (l, 0, 0)),
            pl.BlockSpec((None, 1, HEAD_DIM), lambda b, h: (l, 0, 0)),
        ],
        out_specs=pl.BlockSpec(blk, lambda b, h: (b, 0, h)),
        out_shape=jax.ShapeDtypeStruct((batch, seq, width), BF16),
        scratch_shapes=[pltpu.VMEM((seq, HEAD_DIM), F32) for _ in range(6)],
        compiler_params=_params("parallel", "parallel"),
        name="dilated_attention",
    )(za3, za3, za3, cos_t, sin_t, q_gain, k_gain)
    return out.reshape(n, width)


def _gla_kernel(q_ref, k_ref, v_ref, g_ref, r_ref, wg_ref, bg_ref, og_ref, o_ref, st_ref, o_acc):
    rows = q_ref.shape[0]
    ck = GLA_CHUNK
    dv = HEAD_DIM

    @pl.when(pl.program_id(2) == 0)
    def _():
        st_ref[...] = jnp.zeros_like(st_ref)

    x = _dot(r_ref[...], wg_ref[...]) + bg_ref[...]
    log_a = (jnp.minimum(x, 0.0) - jnp.log1p(jnp.exp(-jnp.abs(x)))) / GLA_GATE_NORMALIZER
    hi = log_a.astype(BF16)
    lo = (log_a - hi.astype(F32)).astype(BF16)
    grp = 4 * ck
    r_i = lax.broadcasted_iota(jnp.int32, (grp, grp), 0)
    c_i = lax.broadcasted_iota(jnp.int32, (grp, grp), 1)
    same = (r_i // ck) == (c_i // ck)
    tri = jnp.where(same & (c_i <= r_i), 1.0, 0.0).astype(BF16)
    ones = jnp.where(same, 1.0, 0.0).astype(BF16)
    b_parts, bl_parts = [], []
    for s in range(rows // grp):
        rs = slice(s * grp, (s + 1) * grp)
        b_parts.append(_dot(tri, hi[rs]) + _dot(tri, lo[rs]))
        bl_parts.append(_dot(ones, hi[rs]) + _dot(ones, lo[rs]))
    b = jnp.concatenate(b_parts, axis=0)
    b_last = jnp.concatenate(bl_parts, axis=0)

    q = q_ref[...].astype(F32) * GLA_DK ** -0.5
    k = k_ref[...].astype(F32)
    q_dec = (q * jnp.exp(b)).astype(BF16)
    k_inv = (k * jnp.exp(-b)).astype(BF16)
    k_dec = (k * jnp.exp(b_last - b)).astype(BF16)
    decay = jnp.exp(b_last)

    lane = lax.broadcasted_iota(jnp.int32, (1, 2 * GLA_DK), 1)
    head0 = lane < GLA_DK
    ar = lax.broadcasted_iota(jnp.int32, (ck, 2 * ck), 0)
    ac = lax.broadcasted_iota(jnp.int32, (ck, 2 * ck), 1)
    causal = (ac % ck) <= ar
    sr = lax.broadcasted_iota(jnp.int32, (2 * dv, 2 * GLA_DK), 0)
    sc = lax.broadcasted_iota(jnp.int32, (2 * dv, 2 * GLA_DK), 1)
    own = (sr < dv) == (sc < GLA_DK)
    zeros_v = jnp.zeros((ck, dv), BF16)

    state = st_ref[...]
    for c in range(rows // ck):
        rs = slice(c * ck, (c + 1) * ck)
        qd = q_dec[rs]
        ki = k_inv[rs]
        ki2 = jnp.concatenate([jnp.where(head0, ki, 0), jnp.where(head0, 0, ki)], axis=0)
        a = jnp.where(causal, _dot_nt(qd, ki2), 0.0)
        vp = v_ref[rs, :]
        v_bd = jnp.concatenate(
            [jnp.concatenate([vp[:, :dv], zeros_v], axis=1),
             jnp.concatenate([zeros_v, vp[:, dv:]], axis=1)], axis=0)
        o_acc[rs, :] = _dot(a.astype(BF16), v_bd) + _dot_nt(qd, state.astype(BF16))
        u = _dot_tn(vp, k_dec[rs])
        state = state * decay[c * ck:c * ck + 1, :] + jnp.where(own, u, 0.0)
    st_ref[...] = state

    for h in range(2):
        cs = slice(h * dv, (h + 1) * dv)
        g = g_ref[:, cs].astype(F32)
        o = _rms_norm(o_acc[:, cs], og_ref[...]) * (g * jax.nn.sigmoid(g))
        o_ref[:, cs] = o.astype(BF16)


def _gla(zg, w_gate, b_gate, out_gain, l, batch):
    n, _ = zg.shape
    seq = n // batch
    dkp = 2 * GLA_DK
    dvp = 2 * HEAD_DIM
    pairs = w_gate.shape[-1] // dkp
    width = pairs * dvp
    rows = min(GLA_ROWS, seq)
    zg3 = zg.reshape(batch, seq, zg.shape[1])
    qk_blocks = 2 * pairs * dkp // dvp
    out = pl.pallas_call(
        _gla_kernel,
        grid=(batch, pairs, seq // rows),
        in_specs=[
            pl.BlockSpec((None, rows, dkp), lambda b, p, t: (b, t, p)),
            pl.BlockSpec((None, rows, dkp), lambda b, p, t: (b, t, pairs + p)),
            pl.BlockSpec((None, rows, dvp), lambda b, p, t: (b, t, qk_blocks + p)),
            pl.BlockSpec((None, rows, dvp), lambda b, p, t: (b, t, qk_blocks + pairs + p)),
            pl.BlockSpec((None, rows, LANES), lambda b, p, t: (b, t, (2 * pairs * dkp + 2 * width) // LANES)),
            pl.BlockSpec((None, LANES, dkp), lambda b, p, t: (l, 0, p)),
            pl.BlockSpec((None, 1, dkp), lambda b, p, t: (l, 0, p)),
            pl.BlockSpec((None, 1, HEAD_DIM), lambda b, p, t: (l, 0, 0)),
        ],
        out_specs=pl.BlockSpec((None, rows, dvp), lambda b, p, t: (b, t, p)),
        out_shape=jax.ShapeDtypeStruct((batch, seq, width), BF16),
        scratch_shapes=[pltpu.VMEM((dvp, dkp), F32), pltpu.VMEM((rows, dvp), F32)],
        compiler_params=_params("parallel", "parallel", "arbitrary"),
        name="gated_linear_attention",
    )(zg3, zg3, zg3, zg3, zg3, w_gate, b_gate, out_gain)
    return out.reshape(n, width)


def _outproj_kernel(x_ref, ya_ref, yb_ref, yc_ref, w_ref, o_ref):
    acc = x_ref[...]
    c = 0
    for y_ref in (ya_ref, yb_ref, yc_ref):
        wdt = y_ref.shape[1]
        acc = acc + _dot(y_ref[...], w_ref[c:c + wdt, :])
        c += wdt
    o_ref[...] = acc


def _outproj(x, ys, w_out, l):
    n, d = x.shape
    tm = min(PROJ_TM, n)
    return pl.pallas_call(
        _outproj_kernel,
        grid=(n // tm,),
        in_specs=[pl.BlockSpec((tm, d), lambda i: (i, 0))]
        + [pl.BlockSpec((tm, y.shape[1]), lambda i: (i, 0)) for y in ys]
        + [pl.BlockSpec((None, w_out.shape[1], d), lambda i: (l, 0, 0), pipeline_mode=pl.Buffered(1))],
        out_specs=pl.BlockSpec((tm, d), lambda i: (i, 0)),
        out_shape=jax.ShapeDtypeStruct((n, d), F32),
        compiler_params=_params("parallel"),
        name="out_proj",
    )(x, *ys, w_out)


def _rope_tables(seq):
    pos = jnp.arange(seq, dtype=F32)
    inv_freq = ROPE_THETA ** (-jnp.arange(0, ROPE_DIMS, 2, dtype=F32) / ROPE_DIMS)
    ang = pos[:, None] * inv_freq[None, :]
    cos, sin = jnp.cos(ang), jnp.sin(ang)
    rest = HEAD_DIM - ROPE_DIMS
    cos_t = jnp.concatenate([cos, cos, jnp.ones((seq, rest), F32)], axis=1)
    sin_t = jnp.concatenate([-sin, sin, jnp.zeros((seq, rest), F32)], axis=1)
    return cos_t, sin_t


def kernel(x, ffn_norm, ffn_w_gate, ffn_w_up, ffn_w_down, mix_norm, w_in, sg_norm, sg_w, sg_b, q_norm, k_norm, gla_w_gate, gla_b_gate, gla_out_norm, w_out):
    batch, seq, d = x.shape
    depth = w_in.shape[0]
    n = batch * seq
    sg_width = sg_norm.shape[-1]
    gla_qk = gla_w_gate.shape[-1]
    gla_width = (gla_qk // GLA_DK) * HEAD_DIM
    att_width = d - sg_width - gla_width
    n_in = w_in.shape[-1]
    assert n_in == 2 * sg_width + 3 * att_width + 2 * gla_qk + 2 * gla_width + GLA_GATE_RANK
    widths = (2 * sg_width, 3 * att_width, 2 * gla_qk + 2 * gla_width + LANES)

    wg, wu, wd = ffn_w_gate.astype(BF16), ffn_w_up.astype(BF16), ffn_w_down.astype(BF16)
    w_in_p = jnp.pad(w_in, ((0, 0), (0, 0), (0, LANES - GLA_GATE_RANK))).astype(BF16)
    w_out_b = w_out.astype(BF16)
    gla_wg = jnp.pad(gla_w_gate, ((0, 0), (0, LANES - GLA_GATE_RANK), (0, 0))).astype(BF16)
    ffn_g = ffn_norm.reshape(depth, 2, 1, d)
    mix_g = mix_norm.reshape(depth, 1, d)
    sg_g = sg_norm.reshape(depth, 1, sg_width)
    sg_bt = jnp.swapaxes(sg_b, 1, 2)
    q_g = q_norm.reshape(depth, 1, HEAD_DIM)
    k_g = k_norm.reshape(depth, 1, HEAD_DIM)
    gla_bg = gla_b_gate.reshape(depth, 1, gla_qk)
    gla_og = gla_out_norm.reshape(depth, 1, HEAD_DIM)
    cos_t, sin_t = _rope_tables(seq)

    h = x.reshape(n, d)
    for l in range(depth):
        h = _ffn(h, ffn_g, wg, wu, wd, l, 0)
        zs, za, zg = _inproj(h, mix_g, w_in_p, l, widths)
        ya = _sgu(zs, sg_g, sg_w, sg_bt, l)
        yb = _attention(za, cos_t, sin_t, q_g, k_g, l, batch)
        yc = _gla(zg, gla_wg, gla_bg, gla_og, l, batch)
        h = _outproj(h, (ya, yb, yc), w_out_b, l)
        h = _ffn(h, ffn_g, wg, wu, wd, l, 1)
    return h.reshape(batch, seq, d)
```

```python
import functools

import jax
import jax.numpy as jnp
from jax import lax
from jax.experimental import pallas as pl
from jax.experimental.pallas import tpu as pltpu

F32 = jnp.float32
BF16 = jnp.bfloat16

EPS = 1e-6
HEAD_DIM = 128
LANES = 128
SG_CHUNK = 128
ATT_DILATIONS = (1, 4, 16)
ATT_STEPS = 128
ATT_RESIDUES = 16
ROPE_THETA = 500000.0
ROPE_DIMS = HEAD_DIM // 4
GLA_DK = 64
GLA_CHUNK = 64
GLA_GATE_RANK = 16
GLA_GATE_NORMALIZER = 16.0
MASK_VALUE = -1e30

VMEM_LIMIT_BYTES = 56 * 1024 * 1024

FFN_TM, FFN_TF = 1024, 512
PROJ_TM = 512
SG_ROWS = 1024
GLA_ROWS = 1024
ATT_PREP_ROWS = 256
ATT_GROUP = 4
LOG2_E = 1.4426950408889634


def _params(*semantics):
    return pltpu.CompilerParams(dimension_semantics=semantics, vmem_limit_bytes=VMEM_LIMIT_BYTES)


def _rms_norm(x, g):
    return x * lax.rsqrt(jnp.mean(x * x, axis=-1, keepdims=True) + EPS) * g


def _dot(a, b):
    return jnp.dot(a, b, preferred_element_type=F32)


def _dot_nt(a, b):
    return lax.dot_general(a, b, (((1,), (1,)), ((), ())), preferred_element_type=F32)


def _dot_tn(a, b):
    return lax.dot_general(a, b, (((0,), (0,)), ((), ())), preferred_element_type=F32)


def _ffn_kernel(x_ref, g_ref, wg_ref, wu_ref, wd_ref, o_ref, h_ref):
    @pl.when(pl.program_id(1) == 0)
    def _():
        x = x_ref[...]
        h_ref[...] = _rms_norm(x, g_ref[...]).astype(BF16)
        o_ref[...] = x

    h = h_ref[...]
    gate = _dot(h, wg_ref[...])
    up = _dot(h, wu_ref[...])
    act = (gate * jax.nn.sigmoid(gate)) * (0.5 * up)
    o_ref[...] += _dot(act.astype(BF16), wd_ref[...])


def _ffn(x, norm_g, w_gate, w_up, w_down, l, j):
    n, d = x.shape
    f = w_gate.shape[-1]
    tm, tf = min(FFN_TM, n), FFN_TF
    return pl.pallas_call(
        _ffn_kernel,
        grid=(n // tm, f // tf),
        in_specs=[
            pl.BlockSpec((tm, d), lambda i, k: (i, 0)),
            pl.BlockSpec((None, None, 1, d), lambda i, k: (l, j, 0, 0)),
            pl.BlockSpec((None, None, d, tf), lambda i, k: (l, j, 0, k)),
            pl.BlockSpec((None, None, d, tf), lambda i, k: (l, j, 0, k)),
            pl.BlockSpec((None, None, tf, d), lambda i, k: (l, j, k, 0)),
        ],
        out_specs=pl.BlockSpec((tm, d), lambda i, k: (i, 0)),
        out_shape=jax.ShapeDtypeStruct((n, d), F32),
        scratch_shapes=[pltpu.VMEM((tm, d), BF16)],
        compiler_params=_params("parallel", "arbitrary"),
        name="ffn",
    )(x, norm_g, w_gate, w_up, w_down)


def _inproj_kernel(x_ref, g_ref, w_ref, zs_ref, za_ref, zg_ref):
    h = _rms_norm(x_ref[...], g_ref[...]).astype(BF16)
    c = 0
    for z_ref in (zs_ref, za_ref, zg_ref):
        wdt = z_ref.shape[1]
        z_ref[...] = _dot(h, w_ref[:, c:c + wdt]).astype(BF16)
        c += wdt


def _inproj(x, norm_g, w_in, l, widths):
    n, d = x.shape
    c = w_in.shape[-1]
    tm = min(PROJ_TM, n)
    return pl.pallas_call(
        _inproj_kernel,
        grid=(n // tm,),
        in_specs=[
            pl.BlockSpec((tm, d), lambda i: (i, 0)),
            pl.BlockSpec((None, 1, d), lambda i: (l, 0, 0)),
            pl.BlockSpec((None, d, c), lambda i: (l, 0, 0), pipeline_mode=pl.Buffered(1)),
        ],
        out_specs=[pl.BlockSpec((tm, w), lambda i: (i, 0)) for w in widths],
        out_shape=[jax.ShapeDtypeStruct((n, w), BF16) for w in widths],
        compiler_params=_params("parallel"),
        name="in_proj",
    )(x, norm_g, w_in)


def _sgu_kernel(z_ref, ng_ref, w_ref, bt_ref, o_ref):
    rows, width = o_ref.shape
    groups = width // HEAD_DIM
    t = SG_CHUNK
    r_i = lax.broadcasted_iota(jnp.int32, (t, t), 0)
    c_i = lax.broadcasted_iota(jnp.int32, (t, t), 1)
    causal = c_i <= r_i
    for g in range(groups):
        cu = slice(g * HEAD_DIM, (g + 1) * HEAD_DIM)
        cv = slice(width + g * HEAD_DIM, width + (g + 1) * HEAD_DIM)
        w = jnp.where(causal, w_ref[g], 0.0).astype(BF16)
        b_col = bt_ref[:, g:g + 1]
        gain = ng_ref[:, cu]
        for c in range(rows // t):
            rs = slice(c * t, (c + 1) * t)
            v = _rms_norm(jax.nn.gelu(z_ref[rs, cv].astype(F32)), gain)
            sv = _dot(w, v.astype(BF16)) + b_col
            u = jax.nn.gelu(z_ref[rs, cu].astype(F32))
            o_ref[rs, cu] = (u * sv).astype(BF16)


def _sgu(zs, sg_norm, sg_w, sg_bt, l):
    n, w2 = zs.shape
    width = w2 // 2
    groups = width // HEAD_DIM
    rows = min(SG_ROWS, n)
    return pl.pallas_call(
        _sgu_kernel,
        grid=(n // rows,),
        in_specs=[
            pl.BlockSpec((rows, w2), lambda i: (i, 0)),
            pl.BlockSpec((None, 1, width), lambda i: (l, 0, 0)),
            pl.BlockSpec((None, groups, SG_CHUNK, SG_CHUNK), lambda i: (l, 0, 0, 0)),
            pl.BlockSpec((None, SG_CHUNK, groups), lambda i: (l, 0, 0)),
        ],
        out_specs=pl.BlockSpec((rows, width), lambda i: (i, 0)),
        out_shape=jax.ShapeDtypeStruct((n, width), BF16),
        compiler_params=_params("parallel"),
        name="spatial_gating",
    )(zs, sg_norm, sg_w, sg_bt)


def _att_kernel(q_ref, k_ref, v_ref, cos_ref, sin_ref, qg_ref, kg_ref, o_ref,
                qn, kn, qd, kd, vd, acc, m_s, l_s, s_buf, mb_buf):
    seq = q_ref.shape[0]
    w = ATT_STEPS
    res = ATT_RESIDUES
    per = seq // res
    pitch = acc.shape[0] // res
    lane = lax.broadcasted_iota(jnp.int32, (1, HEAD_DIM), 1)
    half = ROPE_DIMS // 2
    pr = min(ATT_PREP_ROWS, seq)

    def rope(t, rows):
        partner = jnp.where(lane < half, pltpu.roll(t, HEAD_DIM - half, 1), pltpu.roll(t, half, 1))
        return t * cos_ref[rows, :] + partner * sin_ref[rows, :]

    def stage(src, gain, nat, scale, dst):
        def body(c, carry):
            rows = pl.ds(pl.multiple_of(c * pr, pr), pr)
            t = src[rows, :].astype(F32)
            if gain is not None:
                t = rope(_rms_norm(t, gain[...]), rows) * scale
                nat[rows, :] = t.astype(BF16)
            for j in range(pr // res):
                acc[pl.ds(c * (pr // res) + j, res, stride=pitch), :] = t[j * res:(j + 1) * res, :]
            return carry

        lax.fori_loop(0, seq // pr, body, 0, unroll=2)

        def repack(r, carry):
            dst[r] = acc[pl.ds(pl.multiple_of(r * pitch, 8), per), :].astype(BF16)
            return carry

        lax.fori_loop(0, res, repack, 0)

    stage(q_ref, qg_ref, qn, HEAD_DIM ** -0.5 * LOG2_E, qd)
    stage(k_ref, kg_ref, kn, 1.0, kd)
    stage(v_ref, None, None, None, vd)

    row = lax.broadcasted_iota(jnp.int32, (w, w), 0)
    col = lax.broadcasted_iota(jnp.int32, (w, w), 1)
    group = s_buf.shape[0]

    def run_branch(load_qk, load_v, load_state, store, order_diff):
        def scores(g):
            for u in range(group):
                q, k2, has_prev = load_qk(g * group + u)
                s = _dot_nt(q, k2)
                prev_min = jnp.where(has_prev, 0, 2 * w)
                s = jnp.concatenate([jnp.where(order_diff >= prev_min, s[:, :w], MASK_VALUE),
                                     jnp.where(order_diff <= 0, s[:, w:], MASK_VALUE)], axis=1)
                s_buf[u] = s
                mb_buf[u] = jnp.broadcast_to(jnp.max(s, axis=1, keepdims=True), (w, HEAD_DIM))

        def consume(g):
            for u in range(group):
                i = g * group + u
                old = load_state(i)
                m_blk = mb_buf[u]
                m_new = m_blk if old is None else jnp.maximum(old[0], m_blk)
                p = jnp.exp2(s_buf[u] - jnp.concatenate([m_new, m_new], axis=1))
                p_sum = jnp.sum(p, axis=1, keepdims=True)
                pv = _dot(p.astype(BF16), load_v(i))
                if old is None:
                    store(i, m_new, jnp.broadcast_to(p_sum, (w, HEAD_DIM)), pv)
                else:
                    alpha = jnp.exp2(old[0] - m_new)
                    store(i, m_new, alpha * old[1] + p_sum, alpha * old[2] + pv)

        def body(g, carry):
            consume(g - 1)
            scores(g)
            return carry

        groups = seq // (w * group)
        scores(0)
        lax.fori_loop(1, groups, body, 0)
        consume(groups - 1)

    for d in ATT_DILATIONS[:0:-1]:
        first = d == ATT_DILATIONS[-1]
        pieces = res // d
        plen = w // pieces
        nb = per // plen
        order = lambda x: pieces * (x % plen) + x // plen

        def gather(ref, r, at, d=d, pieces=pieces, plen=plen):
            return jnp.concatenate([ref[c * d + r, pl.ds(at, plen), :] for c in range(pieces)], axis=0)

        def starts(i, nb=nb, plen=plen):
            n = i % nb
            return (i // nb, n, pl.multiple_of(n * plen, plen),
                    pl.multiple_of(jnp.where(n > 0, n - 1, n) * plen, plen))

        def state_rows(r, c, start, d=d, plen=plen):
            return pl.ds(pl.multiple_of((c * d + r) * pitch + start, 8), plen)

        def load_qk(i, gather=gather, starts=starts):
            r, n, start, pstart = starts(i)
            return gather(qd, r, start), jnp.concatenate([gather(kd, r, pstart), gather(kd, r, start)], axis=0), n > 0

        def load_v(i, gather=gather, starts=starts):
            r, _, start, pstart = starts(i)
            return jnp.concatenate([gather(vd, r, pstart), gather(vd, r, start)], axis=0)

        def load_state(i, first=first, pieces=pieces, starts=starts, state_rows=state_rows):
            if first:
                return None
            r, _, start, _ = starts(i)
            return tuple(jnp.concatenate([ref[state_rows(r, c, start), :] for c in range(pieces)], axis=0)
                         for ref in (m_s, l_s, acc))

        def store(i, *new, pieces=pieces, plen=plen, starts=starts, state_rows=state_rows):
            r, _, start, _ = starts(i)
            for ref, val in zip((m_s, l_s, acc), new):
                for c in range(pieces):
                    ref[state_rows(r, c, start), :] = val[c * plen:(c + 1) * plen, :]

        run_branch(load_qk, load_v, load_state, store, order(col) - order(row))

    def nat_starts(n):
        return pl.multiple_of(n * w, w), pl.multiple_of(jnp.where(n > 0, n - 1, n) * w, w)

    def nat_qk(n):
        start, pstart = nat_starts(n)
        return (qn[pl.ds(start, w), :],
                jnp.concatenate([kn[pl.ds(pstart, w), :], kn[pl.ds(start, w), :]], axis=0), n > 0)

    def nat_v(n):
        start, pstart = nat_starts(n)
        return jnp.concatenate([v_ref[pl.ds(pstart, w), :], v_ref[pl.ds(start, w), :]], axis=0)

    def nat_state(n):
        return tuple(jnp.concatenate([ref[pl.ds(n * (w // res) + j, res, stride=pitch), :]
                                      for j in range(w // res)], axis=0) for ref in (m_s, l_s, acc))

    def emit(n, m_new, l_new, acc_new):
        o_ref[pl.ds(nat_starts(n)[0], w), :] = (acc_new / l_new).astype(BF16)

    run_branch(nat_qk, nat_v, nat_state, emit, col - row)


def _attention(za, cos_t, sin_t, q_gain, k_gain, l, batch):
    n, w3 = za.shape
    seq = n // batch
    width = w3 // 3
    heads = width // HEAD_DIM
    assert seq % (ATT_STEPS * ATT_RESIDUES) == 0
    per = seq // ATT_RESIDUES
    pitch = per + 8
    za3 = za.reshape(batch, seq, w3)
    blk = (None, seq, HEAD_DIM)
    const = functools.partial(pl.BlockSpec, pipeline_mode=pl.Buffered(1))
    out = pl.pallas_call(
        _att_kernel,
        grid=(batch, heads),
        in_specs=[
            pl.BlockSpec(blk, lambda b, h: (b, 0, h)),
            pl.BlockSpec(blk, lambda b, h: (b, 0, heads + h)),
            pl.BlockSpec(blk, lambda b, h: (b, 0, 2 * heads + h)),
            const((seq, HEAD_DIM), lambda b, h: (0, 0)),
            const((seq, HEAD_DIM), lambda b, h: (0, 0)),
            pl.BlockSpec((None, 1, HEAD_DIM), lambda b, h: (l, 0, 0)),
            pl.BlockSpec((None, 1, HEAD_DIM), lambda b, h: (l, 0, 0)),
        ],
        out_specs=pl.BlockSpec(blk, lambda b, h: (b, 0, h)),
        out_shape=jax.ShapeDtypeStruct((batch, seq, width), BF16),
        scratch_shapes=[pltpu.VMEM((seq, HEAD_DIM), BF16) for _ in range(2)]
        + [pltpu.VMEM((ATT_RESIDUES, per, HEAD_DIM), BF16) for _ in range(3)]
        + [pltpu.VMEM((ATT_RESIDUES * pitch, HEAD_DIM), F32) for _ in range(3)]
        + [pltpu.VMEM((ATT_GROUP, ATT_STEPS, 2 * ATT_STEPS), F32), pltpu.VMEM((ATT_GROUP, ATT_STEPS, HEAD_DIM), F32)],
        compiler_params=_params("parallel", "parallel"),
        name="dilated_attention",
    )(za3, za3, za3, cos_t, sin_t, q_gain, k_gain)
    return out.reshape(n, width)


def _gla_kernel(q_ref, k_ref, v_ref, g_ref, r_ref, wg_ref, bg_ref, og_ref, o_ref, st_ref, o_acc):
    rows = q_ref.shape[0]
    ck = GLA_CHUNK
    dv = HEAD_DIM

    @pl.when(pl.program_id(2) == 0)
    def _():
        st_ref[...] = jnp.zeros_like(st_ref)

    x = _dot(r_ref[...], wg_ref[...]) + bg_ref[...]
    log_a = (jnp.minimum(x, 0.0) - jnp.log1p(jnp.exp(-jnp.abs(x)))) / GLA_GATE_NORMALIZER
    hi = log_a.astype(BF16)
    lo = (log_a - hi.astype(F32)).astype(BF16)
    grp = 4 * ck
    r_i = lax.broadcasted_iota(jnp.int32, (grp, grp), 0)
    c_i = lax.broadcasted_iota(jnp.int32, (grp, grp), 1)
    same = (r_i // ck) == (c_i // ck)
    tri = jnp.where(same & (c_i <= r_i), 1.0, 0.0).astype(BF16)
    ones = jnp.where(same, 1.0, 0.0).astype(BF16)
    b_parts, bl_parts = [], []
    for s in range(rows // grp):
        rs = slice(s * grp, (s + 1) * grp)
        b_parts.append(_dot(tri, hi[rs]) + _dot(tri, lo[rs]))
        bl_parts.append(_dot(ones, hi[rs]) + _dot(ones, lo[rs]))
    b = jnp.concatenate(b_parts, axis=0)
    b_last = jnp.concatenate(bl_parts, axis=0)

    q = q_ref[...].astype(F32) * GLA_DK ** -0.5
    k = k_ref[...].astype(F32)
    q_dec = (q * jnp.exp(b)).astype(BF16)
    k_inv = (k * jnp.exp(-b)).astype(BF16)
    k_dec = (k * jnp.exp(b_last - b)).astype(BF16)
    decay = jnp.exp(b_last)

    lane = lax.broadcasted_iota(jnp.int32, (1, 2 * GLA_DK), 1)
    head0 = lane < GLA_DK
    ar = lax.broadcasted_iota(jnp.int32, (ck, 2 * ck), 0)
    ac = lax.broadcasted_iota(jnp.int32, (ck, 2 * ck), 1)
    causal = (ac % ck) <= ar
    sr = lax.broadcasted_iota(jnp.int32, (2 * dv, 2 * GLA_DK), 0)
    sc = lax.broadcasted_iota(jnp.int32, (2 * dv, 2 * GLA_DK), 1)
    own = (sr < dv) == (sc < GLA_DK)
    zeros_v = jnp.zeros((ck, dv), BF16)

    state = st_ref[...]
    for c in range(rows // ck):
        rs = slice(c * ck, (c + 1) * ck)
        qd = q_dec[rs]
        ki = k_inv[rs]
        ki2 = jnp.concatenate([jnp.where(head0, ki, 0), jnp.where(head0, 0, ki)], axis=0)
        a = jnp.where(causal, _dot_nt(qd, ki2), 0.0)
        vp = v_ref[rs, :]
        v_bd = jnp.concatenate(
            [jnp.concatenate([vp[:, :dv], zeros_v], axis=1),
             jnp.concatenate([zeros_v, vp[:, dv:]], axis=1)], axis=0)
        o_acc[rs, :] = _dot(a.astype(BF16), v_bd) + _dot_nt(qd, state.astype(BF16))
        u = _dot_tn(vp, k_dec[rs])
        state = state * decay[c * ck:c * ck + 1, :] + jnp.where(own, u, 0.0)
    st_ref[...] = state

    for h in range(2):
        cs = slice(h * dv, (h + 1) * dv)
        g = g_ref[:, cs].astype(F32)
        o = _rms_norm(o_acc[:, cs], og_ref[...]) * (g * jax.nn.sigmoid(g))
        o_ref[:, cs] = o.astype(BF16)


def _gla(zg, w_gate, b_gate, out_gain, l, batch):
    n, _ = zg.shape
    seq = n // batch
    dkp = 2 * GLA_DK
    dvp = 2 * HEAD_DIM
    pairs = w_gate.shape[-1] // dkp
    width = pairs * dvp
    rows = min(GLA_ROWS, seq)
    zg3 = zg.reshape(batch, seq, zg.shape[1])
    qk_blocks = 2 * pairs * dkp // dvp
    out = pl.pallas_call(
        _gla_kernel,
        grid=(batch, pairs, seq // rows),
        in_specs=[
            pl.BlockSpec((None, rows, dkp), lambda b, p, t: (b, t, p)),
            pl.BlockSpec((None, rows, dkp), lambda b, p, t: (b, t, pairs + p)),
            pl.BlockSpec((None, rows, dvp), lambda b, p, t: (b, t, qk_blocks + p)),
            pl.BlockSpec((None, rows, dvp), lambda b, p, t: (b, t, qk_blocks + pairs + p)),
            pl.BlockSpec((None, rows, LANES), lambda b, p, t: (b, t, (2 * pairs * dkp + 2 * width) // LANES)),
            pl.BlockSpec((None, LANES, dkp), lambda b, p, t: (l, 0, p)),
            pl.BlockSpec((None, 1, dkp), lambda b, p, t: (l, 0, p)),
            pl.BlockSpec((None, 1, HEAD_DIM), lambda b, p, t: (l, 0, 0)),
        ],
        out_specs=pl.BlockSpec((None, rows, dvp), lambda b, p, t: (b, t, p)),
        out_shape=jax.ShapeDtypeStruct((batch, seq, width), BF16),
        scratch_shapes=[pltpu.VMEM((dvp, dkp), F32), pltpu.VMEM((rows, dvp), F32)],
        compiler_params=_params("parallel", "parallel", "arbitrary"),
        name="gated_linear_attention",
    )(zg3, zg3, zg3, zg3, zg3, w_gate, b_gate, out_gain)
    return out.reshape(n, width)


def _outproj_kernel(x_ref, ya_ref, yb_ref, yc_ref, w_ref, o_ref):
    acc = x_ref[...]
    c = 0
    for y_ref in (ya_ref, yb_ref, yc_ref):
        wdt = y_ref.shape[1]
        acc = acc + _dot(y_ref[...], w_ref[c:c + wdt, :])
        c += wdt
    o_ref[...] = acc


def _outproj(x, ys, w_out, l):
    n, d = x.shape
    tm = min(PROJ_TM, n)
    return pl.pallas_call(
        _outproj_kernel,
        grid=(n // tm,),
        in_specs=[pl.BlockSpec((tm, d), lambda i: (i, 0))]
        + [pl.BlockSpec((tm, y.shape[1]), lambda i: (i, 0)) for y in ys]
        + [pl.BlockSpec((None, w_out.shape[1], d), lambda i: (l, 0, 0), pipeline_mode=pl.Buffered(1))],
        out_specs=pl.BlockSpec((tm, d), lambda i: (i, 0)),
        out_shape=jax.ShapeDtypeStruct((n, d), F32),
        compiler_params=_params("parallel"),
        name="out_proj",
    )(x, *ys, w_out)


def _rope_tables(seq):
    pos = jnp.arange(seq, dtype=F32)
    inv_freq = ROPE_THETA ** (-jnp.arange(0, ROPE_DIMS, 2, dtype=F32) / ROPE_DIMS)
    ang = pos[:, None] * inv_freq[None, :]
    cos, sin = jnp.cos(ang), jnp.sin(ang)
    rest = HEAD_DIM - ROPE_DIMS
    cos_t = jnp.concatenate([cos, cos, jnp.ones((seq, rest), F32)], axis=1)
    sin_t = jnp.concatenate([-sin, sin, jnp.zeros((seq, rest), F32)], axis=1)
    return cos_t, sin_t


def kernel(x, ffn_norm, ffn_w_gate, ffn_w_up, ffn_w_down, mix_norm, w_in, sg_norm, sg_w, sg_b, q_norm, k_norm, gla_w_gate, gla_b_gate, gla_out_norm, w_out):
    batch, seq, d = x.shape
    depth = w_in.shape[0]
    n = batch * seq
    sg_width = sg_norm.shape[-1]
    gla_qk = gla_w_gate.shape[-1]
    gla_width = (gla_qk // GLA_DK) * HEAD_DIM
    att_width = d - sg_width - gla_width
    n_in = w_in.shape[-1]
    assert n_in == 2 * sg_width + 3 * att_width + 2 * gla_qk + 2 * gla_width + GLA_GATE_RANK
    widths = (2 * sg_width, 3 * att_width, 2 * gla_qk + 2 * gla_width + LANES)

    wg, wu, wd = ffn_w_gate.astype(BF16), ffn_w_up.astype(BF16), ffn_w_down.astype(BF16)
    w_in_p = jnp.pad(w_in, ((0, 0), (0, 0), (0, LANES - GLA_GATE_RANK))).astype(BF16)
    w_out_b = w_out.astype(BF16)
    gla_wg = jnp.pad(gla_w_gate, ((0, 0), (0, LANES - GLA_GATE_RANK), (0, 0))).astype(BF16)
    ffn_g = ffn_norm.reshape(depth, 2, 1, d)
    mix_g = mix_norm.reshape(depth, 1, d)
    sg_g = sg_norm.reshape(depth, 1, sg_width)
    sg_bt = jnp.swapaxes(sg_b, 1, 2)
    q_g = q_norm.reshape(depth, 1, HEAD_DIM)
    k_g = k_norm.reshape(depth, 1, HEAD_DIM)
    gla_bg = gla_b_gate.reshape(depth, 1, gla_qk)
    gla_og = gla_out_norm.reshape(depth, 1, HEAD_DIM)
    cos_t, sin_t = _rope_tables(seq)

    h = x.reshape(n, d)
    for l in range(depth):
        h = _ffn(h, ffn_g, wg, wu, wd, l, 0)
        zs, za, zg = _inproj(h, mix_g, w_in_p, l, widths)
        ya = _sgu(zs, sg_g, sg_w, sg_bt, l)
        yb = _attention(za, cos_t, sin_t, q_g, k_g, l, batch)
        yc = _gla(zg, gla_wg, gla_bg, gla_og, l, batch)
        h = _outproj(h, (ya, yb, yc), w_out_b, l)
        h = _ffn(h, ffn_g, wg, wu, wd, l, 1)
    return h.reshape(batch, seq, d)
```

```python
import functools

import jax
import jax.numpy as jnp
from jax import lax
from jax.experimental import pallas as pl
from jax.experimental.pallas import tpu as pltpu

F32 = jnp.float32
BF16 = jnp.bfloat16

EPS = 1e-6
HEAD_DIM = 128
LANES = 128
SG_CHUNK = 128
ATT_DILATIONS = (1, 4, 16)
ATT_STEPS = 128
ATT_RESIDUES = 16
ROPE_THETA = 500000.0
ROPE_DIMS = HEAD_DIM // 4
GLA_DK = 64
GLA_CHUNK = 64
GLA_GATE_RANK = 16
GLA_GATE_NORMALIZER = 16.0
MASK_VALUE = -1e30

VMEM_LIMIT_BYTES = 56 * 1024 * 1024

FFN_TM, FFN_TF = 1024, 512
PROJ_TM = 512
SG_ROWS = 1024
GLA_ROWS = 1024
ATT_PREP_ROWS = 256
ATT_GROUP = 4
LOG2_E = 1.4426950408889634


def _params(*semantics):
    return pltpu.CompilerParams(dimension_semantics=semantics, vmem_limit_bytes=VMEM_LIMIT_BYTES)


def _rms_norm(x, g):
    return x * lax.rsqrt(jnp.mean(x * x, axis=-1, keepdims=True) + EPS) * g


def _dot(a, b):
    return jnp.dot(a, b, preferred_element_type=F32)


def _dot_nt(a, b):
    return lax.dot_general(a, b, (((1,), (1,)), ((), ())), preferred_element_type=F32)


def _dot_tn(a, b):
    return lax.dot_general(a, b, (((0,), (0,)), ((), ())), preferred_element_type=F32)


def _ffn_kernel(x_ref, g_ref, wg_ref, wu_ref, wd_ref, o_ref, h_ref):
    @pl.when(pl.program_id(1) == 0)
    def _():
        x = x_ref[...]
        h_ref[...] = _rms_norm(x, g_ref[...]).astype(BF16)
        o_ref[...] = x

    h = h_ref[...]
    gate = _dot(h, wg_ref[...])
    up = _dot(h, wu_ref[...])
    act = (gate * jax.nn.sigmoid(gate)) * (0.5 * up)
    o_ref[...] += _dot(act.astype(BF16), wd_ref[...])


def _ffn(x, norm_g, w_gate, w_up, w_down, l, j):
    n, d = x.shape
    f = w_gate.shape[-1]
    tm, tf = min(FFN_TM, n), FFN_TF
    return pl.pallas_call(
        _ffn_kernel,
        grid=(n // tm, f // tf),
        in_specs=[
            pl.BlockSpec((tm, d), lambda i, k: (i, 0)),
            pl.BlockSpec((None, None, 1, d), lambda i, k: (l, j, 0, 0)),
            pl.BlockSpec((None, None, d, tf), lambda i, k: (l, j, 0, k)),
            pl.BlockSpec((None, None, d, tf), lambda i, k: (l, j, 0, k)),
            pl.BlockSpec((None, None, tf, d), lambda i, k: (l, j, k, 0)),
        ],
        out_specs=pl.BlockSpec((tm, d), lambda i, k: (i, 0)),
        out_shape=jax.ShapeDtypeStruct((n, d), F32),
        scratch_shapes=[pltpu.VMEM((tm, d), BF16)],
        compiler_params=_params("parallel", "arbitrary"),
        name="ffn",
    )(x, norm_g, w_gate, w_up, w_down)


def _inproj_kernel(x_ref, g_ref, w_ref, cos_ref, sin_ref, qg_ref, kg_ref, zs_ref, za_ref, zg_ref):
    h = _rms_norm(x_ref[...], g_ref[...]).astype(BF16)
    c0 = zs_ref.shape[1]
    c1 = c0 + za_ref.shape[1]
    zs_ref[...] = _dot(h, w_ref[:, :c0]).astype(BF16)

    za = _dot(h, w_ref[:, c0:c1])
    heads = za_ref.shape[1] // (3 * HEAD_DIM)
    lane = lax.broadcasted_iota(jnp.int32, (1, HEAD_DIM), 1)
    half = ROPE_DIMS // 2
    cos = cos_ref[...]
    sin = sin_ref[...]
    for j in range(2 * heads):
        cols = slice(j * HEAD_DIM, (j + 1) * HEAD_DIM)
        gain, scale = (qg_ref, HEAD_DIM ** -0.5 * LOG2_E) if j < heads else (kg_ref, None)
        t = _rms_norm(za[:, cols], gain[...])
        partner = jnp.where(lane < half, pltpu.roll(t, HEAD_DIM - half, 1), pltpu.roll(t, half, 1))
        t = t * cos + partner * sin
        if scale is not None:
            t = t * scale
        za_ref[:, cols] = t.astype(BF16)
    za_ref[:, 2 * heads * HEAD_DIM:] = za[:, 2 * heads * HEAD_DIM:].astype(BF16)

    zg_ref[...] = _dot(h, w_ref[:, c1:]).astype(BF16)


def _inproj(x, norm_g, w_in, cos_t, sin_t, q_gain, k_gain, l, widths):
    n, d = x.shape
    c = w_in.shape[-1]
    seq = cos_t.shape[0]
    tm = min(PROJ_TM, seq)
    assert seq % tm == 0
    return pl.pallas_call(
        _inproj_kernel,
        grid=(n // tm,),
        in_specs=[
            pl.BlockSpec((tm, d), lambda i: (i, 0)),
            pl.BlockSpec((None, 1, d), lambda i: (l, 0, 0)),
            pl.BlockSpec((None, d, c), lambda i: (l, 0, 0), pipeline_mode=pl.Buffered(1)),
            pl.BlockSpec((tm, HEAD_DIM), lambda i: (i % (seq // tm), 0)),
            pl.BlockSpec((tm, HEAD_DIM), lambda i: (i % (seq // tm), 0)),
            pl.BlockSpec((None, 1, HEAD_DIM), lambda i: (l, 0, 0)),
            pl.BlockSpec((None, 1, HEAD_DIM), lambda i: (l, 0, 0)),
        ],
        out_specs=[pl.BlockSpec((tm, w), lambda i: (i, 0)) for w in widths],
        out_shape=[jax.ShapeDtypeStruct((n, w), BF16) for w in widths],
        compiler_params=_params("parallel"),
        name="in_proj",
    )(x, norm_g, w_in, cos_t, sin_t, q_gain, k_gain)


def _sgu_kernel(z_ref, ng_ref, w_ref, bt_ref, o_ref):
    rows, width = o_ref.shape
    groups = width // HEAD_DIM
    t = SG_CHUNK
    r_i = lax.broadcasted_iota(jnp.int32, (t, t), 0)
    c_i = lax.broadcasted_iota(jnp.int32, (t, t), 1)
    causal = c_i <= r_i
    for g in range(groups):
        cu = slice(g * HEAD_DIM, (g + 1) * HEAD_DIM)
        cv = slice(width + g * HEAD_DIM, width + (g + 1) * HEAD_DIM)
        w = jnp.where(causal, w_ref[g], 0.0).astype(BF16)
        b_col = bt_ref[:, g:g + 1]
        gain = ng_ref[:, cu]
        for c in range(rows // t):
            rs = slice(c * t, (c + 1) * t)
            v = _rms_norm(jax.nn.gelu(z_ref[rs, cv].astype(F32)), gain)
            sv = _dot(w, v.astype(BF16)) + b_col
            u = jax.nn.gelu(z_ref[rs, cu].astype(F32))
            o_ref[rs, cu] = (u * sv).astype(BF16)


def _sgu(zs, sg_norm, sg_w, sg_bt, l):
    n, w2 = zs.shape
    width = w2 // 2
    groups = width // HEAD_DIM
    rows = min(SG_ROWS, n)
    return pl.pallas_call(
        _sgu_kernel,
        grid=(n // rows,),
        in_specs=[
            pl.BlockSpec((rows, w2), lambda i: (i, 0)),
            pl.BlockSpec((None, 1, width), lambda i: (l, 0, 0)),
            pl.BlockSpec((None, groups, SG_CHUNK, SG_CHUNK), lambda i: (l, 0, 0, 0)),
            pl.BlockSpec((None, SG_CHUNK, groups), lambda i: (l, 0, 0)),
        ],
        out_specs=pl.BlockSpec((rows, width), lambda i: (i, 0)),
        out_shape=jax.ShapeDtypeStruct((n, width), BF16),
        compiler_params=_params("parallel"),
        name="spatial_gating",
    )(zs, sg_norm, sg_w, sg_bt)


def _att_kernel(qn, kn, v_ref, o_ref, qd, kd, vd, acc, m_s, l_s, s_buf, mb_buf):
    seq = qn.shape[0]
    w = ATT_STEPS
    res = ATT_RESIDUES
    per = seq // res
    pitch = acc.shape[0] // res
    pr = min(ATT_PREP_ROWS, seq)

    def stage(src, dst):
        def body(c, carry):
            t = src[pl.ds(pl.multiple_of(c * pr, pr), pr), :].astype(F32)
            for j in range(pr // res):
                acc[pl.ds(c * (pr // res) + j, res, stride=pitch), :] = t[j * res:(j + 1) * res, :]
            return carry

        lax.fori_loop(0, seq // pr, body, 0, unroll=2)

        def repack(r, carry):
            dst[r] = acc[pl.ds(pl.multiple_of(r * pitch, 8), per), :].astype(BF16)
            return carry

        lax.fori_loop(0, res, repack, 0)

    stage(qn, qd)
    stage(kn, kd)
    stage(v_ref, vd)

    row = lax.broadcasted_iota(jnp.int32, (w, w), 0)
    col = lax.broadcasted_iota(jnp.int32, (w, w), 1)
    group = s_buf.shape[0]

    def run_branch(load_qk, load_v, load_state, store, order_diff):
        def scores(g):
            for u in range(group):
                q, k2, has_prev = load_qk(g * group + u)
                s = _dot_nt(q, k2)
                prev_min = jnp.where(has_prev, 0, 2 * w)
                s = jnp.concatenate([jnp.where(order_diff >= prev_min, s[:, :w], MASK_VALUE),
                                     jnp.where(order_diff <= 0, s[:, w:], MASK_VALUE)], axis=1)
                s_buf[u] = s
                mb_buf[u] = jnp.broadcast_to(jnp.max(s, axis=1, keepdims=True), (w, HEAD_DIM))

        def consume(g):
            for u in range(group):
                i = g * group + u
                old = load_state(i)
                m_blk = mb_buf[u]
                m_new = m_blk if old is None else jnp.maximum(old[0], m_blk)
                p = jnp.exp2(s_buf[u] - jnp.concatenate([m_new, m_new], axis=1))
                p_sum = jnp.sum(p, axis=1, keepdims=True)
                pv = _dot(p.astype(BF16), load_v(i))
                if old is None:
                    store(i, m_new, jnp.broadcast_to(p_sum, (w, HEAD_DIM)), pv)
                else:
                    alpha = jnp.exp2(old[0] - m_new)
                    store(i, m_new, alpha * old[1] + p_sum, alpha * old[2] + pv)

        def body(g, carry):
            consume(g - 1)
            scores(g)
            return carry

        groups = seq // (w * group)
        scores(0)
        lax.fori_loop(1, groups, body, 0)
        consume(groups - 1)

    for d in ATT_DILATIONS[:0:-1]:
        first = d == ATT_DILATIONS[-1]
        pieces = res // d
        plen = w // pieces
        nb = per // plen
        order = lambda x: pieces * (x % plen) + x // plen

        def gather(ref, r, at, d=d, pieces=pieces, plen=plen):
            return jnp.concatenate([ref[c * d + r, pl.ds(at, plen), :] for c in range(pieces)], axis=0)

        def starts(i, nb=nb, plen=plen):
            n = i % nb
            return (i // nb, n, pl.multiple_of(n * plen, plen),
                    pl.multiple_of(jnp.where(n > 0, n - 1, n) * plen, plen))

        def state_rows(r, c, start, d=d, plen=plen):
            return pl.ds(pl.multiple_of((c * d + r) * pitch + start, 8), plen)

        def load_qk(i, gather=gather, starts=starts):
            r, n, start, pstart = starts(i)
            return gather(qd, r, start), jnp.concatenate([gather(kd, r, pstart), gather(kd, r, start)], axis=0), n > 0

        def load_v(i, gather=gather, starts=starts):
            r, _, start, pstart = starts(i)
            return jnp.concatenate([gather(vd, r, pstart), gather(vd, r, start)], axis=0)

        def load_state(i, first=first, pieces=pieces, starts=starts, state_rows=state_rows):
            if first:
                return None
            r, _, start, _ = starts(i)
            return tuple(jnp.concatenate([ref[state_rows(r, c, start), :] for c in range(pieces)], axis=0)
                         for ref in (m_s, l_s, acc))

        def store(i, *new, pieces=pieces, plen=plen, starts=starts, state_rows=state_rows):
            r, _, start, _ = starts(i)
            for ref, val in zip((m_s, l_s, acc), new):
                for c in range(pieces):
                    ref[state_rows(r, c, start), :] = val[c * plen:(c + 1) * plen, :]

        run_branch(load_qk, load_v, load_state, store, order(col) - order(row))

    def nat_starts(n):
        return pl.multiple_of(n * w, w), pl.multiple_of(jnp.where(n > 0, n - 1, n) * w, w)

    def nat_qk(n):
        start, pstart = nat_starts(n)
        return (qn[pl.ds(start, w), :],
                jnp.concatenate([kn[pl.ds(pstart, w), :], kn[pl.ds(start, w), :]], axis=0), n > 0)

    def nat_v(n):
        start, pstart = nat_starts(n)
        return jnp.concatenate([v_ref[pl.ds(pstart, w), :], v_ref[pl.ds(start, w), :]], axis=0)

    def nat_state(n):
        return tuple(jnp.concatenate([ref[pl.ds(n * (w // res) + j, res, stride=pitch), :]
                                      for j in range(w // res)], axis=0) for ref in (m_s, l_s, acc))

    def emit(n, m_new, l_new, acc_new):
        o_ref[pl.ds(nat_starts(n)[0], w), :] = (acc_new / l_new).astype(BF16)

    run_branch(nat_qk, nat_v, nat_state, emit, col - row)


def _attention(za, batch):
    n, w3 = za.shape
    seq = n // batch
    width = w3 // 3
    heads = width // HEAD_DIM
    assert seq % (ATT_STEPS * ATT_RESIDUES) == 0
    per = seq // ATT_RESIDUES
    pitch = per + 8
    za3 = za.reshape(batch, seq, w3)
    blk = (None, seq, HEAD_DIM)
    out = pl.pallas_call(
        _att_kernel,
        grid=(batch, heads),
        in_specs=[
            pl.BlockSpec(blk, lambda b, h: (b, 0, h)),
            pl.BlockSpec(blk, lambda b, h: (b, 0, heads + h)),
            pl.BlockSpec(blk, lambda b, h: (b, 0, 2 * heads + h)),
        ],
        out_specs=pl.BlockSpec(blk, lambda b, h: (b, 0, h)),
        out_shape=jax.ShapeDtypeStruct((batch, seq, width), BF16),
        scratch_shapes=[pltpu.VMEM((ATT_RESIDUES, per, HEAD_DIM), BF16) for _ in range(3)]
        + [pltpu.VMEM((ATT_RESIDUES * pitch, HEAD_DIM), F32) for _ in range(3)]
        + [pltpu.VMEM((ATT_GROUP, ATT_STEPS, 2 * ATT_STEPS), F32), pltpu.VMEM((ATT_GROUP, ATT_STEPS, HEAD_DIM), F32)],
        compiler_params=_params("parallel", "parallel"),
        name="dilated_attention",
    )(za3, za3, za3)
    return out.reshape(n, width)


def _gla_kernel(q_ref, k_ref, v_ref, g_ref, r_ref, wg_ref, bg_ref, og_ref, o_ref, st_ref, o_acc):
    rows = q_ref.shape[0]
    ck = GLA_CHUNK
    dv = HEAD_DIM

    @pl.when(pl.program_id(2) == 0)
    def _():
        st_ref[...] = jnp.zeros_like(st_ref)

    x = _dot(r_ref[...], wg_ref[...]) + bg_ref[...]
    log_a = (jnp.minimum(x, 0.0) - jnp.log1p(jnp.exp(-jnp.abs(x)))) / GLA_GATE_NORMALIZER
    hi = log_a.astype(BF16)
    lo = (log_a - hi.astype(F32)).astype(BF16)
    grp = 4 * ck
    r_i = lax.broadcasted_iota(jnp.int32, (grp, grp), 0)
    c_i = lax.broadcasted_iota(jnp.int32, (grp, grp), 1)
    same = (r_i // ck) == (c_i // ck)
    tri = jnp.where(same & (c_i <= r_i), 1.0, 0.0).astype(BF16)
    ones = jnp.where(same, 1.0, 0.0).astype(BF16)
    b_parts, bl_parts = [], []
    for s in range(rows // grp):
        rs = slice(s * grp, (s + 1) * grp)
        b_parts.append(_dot(tri, hi[rs]) + _dot(tri, lo[rs]))
        bl_parts.append(_dot(ones, hi[rs]) + _dot(ones, lo[rs]))
    b = jnp.concatenate(b_parts, axis=0)
    b_last = jnp.concatenate(bl_parts, axis=0)

    q = q_ref[...].astype(F32) * GLA_DK ** -0.5
    k = k_ref[...].astype(F32)
    q_dec = (q * jnp.exp(b)).astype(BF16)
    k_inv = (k * jnp.exp(-b)).astype(BF16)
    k_dec = (k * jnp.exp(b_last - b)).astype(BF16)
    decay = jnp.exp(b_last)

    lane = lax.broadcasted_iota(jnp.int32, (1, 2 * GLA_DK), 1)
    head0 = lane < GLA_DK
    ar = lax.broadcasted_iota(jnp.int32, (ck, 2 * ck), 0)
    ac = lax.broadcasted_iota(jnp.int32, (ck, 2 * ck), 1)
    causal = (ac % ck) <= ar
    sr = lax.broadcasted_iota(jnp.int32, (2 * dv, 2 * GLA_DK), 0)
    sc = lax.broadcasted_iota(jnp.int32, (2 * dv, 2 * GLA_DK), 1)
    own = (sr < dv) == (sc < GLA_DK)
    zeros_v = jnp.zeros((ck, dv), BF16)

    state = st_ref[...]
    for c in range(rows // ck):
        rs = slice(c * ck, (c + 1) * ck)
        qd = q_dec[rs]
        ki = k_inv[rs]
        ki2 = jnp.concatenate([jnp.where(head0, ki, 0), jnp.where(head0, 0, ki)], axis=0)
        a = jnp.where(causal, _dot_nt(qd, ki2), 0.0)
        vp = v_ref[rs, :]
        v_bd = jnp.concatenate(
            [jnp.concatenate([vp[:, :dv], zeros_v], axis=1),
             jnp.concatenate([zeros_v, vp[:, dv:]], axis=1)], axis=0)
        o_acc[rs, :] = _dot(a.astype(BF16), v_bd) + _dot_nt(qd, state.astype(BF16))
        u = _dot_tn(vp, k_dec[rs])
        state = state * decay[c * ck:c * ck + 1, :] + jnp.where(own, u, 0.0)
    st_ref[...] = state

    for h in range(2):
        cs = slice(h * dv, (h + 1) * dv)
        g = g_ref[:, cs].astype(F32)
        o = _rms_norm(o_acc[:, cs], og_ref[...]) * (g * jax.nn.sigmoid(g))
        o_ref[:, cs] = o.astype(BF16)


def _gla(zg, w_gate, b_gate, out_gain, l, batch):
    n, _ = zg.shape
    seq = n // batch
    dkp = 2 * GLA_DK
    dvp = 2 * HEAD_DIM
    pairs = w_gate.shape[-1] // dkp
    width = pairs * dvp
    rows = min(GLA_ROWS, seq)
    zg3 = zg.reshape(batch, seq, zg.shape[1])
    qk_blocks = 2 * pairs * dkp // dvp
    out = pl.pallas_call(
        _gla_kernel,
        grid=(batch, pairs, seq // rows),
        in_specs=[
            pl.BlockSpec((None, rows, dkp), lambda b, p, t: (b, t, p)),
            pl.BlockSpec((None, rows, dkp), lambda b, p, t: (b, t, pairs + p)),
            pl.BlockSpec((None, rows, dvp), lambda b, p, t: (b, t, qk_blocks + p)),
            pl.BlockSpec((None, rows, dvp), lambda b, p, t: (b, t, qk_blocks + pairs + p)),
            pl.BlockSpec((None, rows, LANES), lambda b, p, t: (b, t, (2 * pairs * dkp + 2 * width) // LANES)),
            pl.BlockSpec((None, LANES, dkp), lambda b, p, t: (l, 0, p)),
            pl.BlockSpec((None, 1, dkp), lambda b, p, t: (l, 0, p)),
            pl.BlockSpec((None, 1, HEAD_DIM), lambda b, p, t: (l, 0, 0)),
        ],
        out_specs=pl.BlockSpec((None, rows, dvp), lambda b, p, t: (b, t, p)),
        out_shape=jax.ShapeDtypeStruct((batch, seq, width), BF16),
        scratch_shapes=[pltpu.VMEM((dvp, dkp), F32), pltpu.VMEM((rows, dvp), F32)],
        compiler_params=_params("parallel", "parallel", "arbitrary"),
        name="gated_linear_attention",
    )(zg3, zg3, zg3, zg3, zg3, w_gate, b_gate, out_gain)
    return out.reshape(n, width)


def _outproj_kernel(x_ref, ya_ref, yb_ref, yc_ref, w_ref, o_ref):
    acc = x_ref[...]
    c = 0
    for y_ref in (ya_ref, yb_ref, yc_ref):
        wdt = y_ref.shape[1]
        acc = acc + _dot(y_ref[...], w_ref[c:c + wdt, :])
        c += wdt
    o_ref[...] = acc


def _outproj(x, ys, w_out, l):
    n, d = x.shape
    tm = min(PROJ_TM, n)
    return pl.pallas_call(
        _outproj_kernel,
        grid=(n // tm,),
        in_specs=[pl.BlockSpec((tm, d), lambda i: (i, 0))]
        + [pl.BlockSpec((tm, y.shape[1]), lambda i: (i, 0)) for y in ys]
        + [pl.BlockSpec((None, w_out.shape[1], d), lambda i: (l, 0, 0), pipeline_mode=pl.Buffered(1))],
        out_specs=pl.BlockSpec((tm, d), lambda i: (i, 0)),
        out_shape=jax.ShapeDtypeStruct((n, d), F32),
        compiler_params=_params("parallel"),
        name="out_proj",
    )(x, *ys, w_out)


def _rope_tables(seq):
    pos = jnp.arange(seq, dtype=F32)
    inv_freq = ROPE_THETA ** (-jnp.arange(0, ROPE_DIMS, 2, dtype=F32) / ROPE_DIMS)
    ang = pos[:, None] * inv_freq[None, :]
    cos, sin = jnp.cos(ang), jnp.sin(ang)
    rest = HEAD_DIM - ROPE_DIMS
    cos_t = jnp.concatenate([cos, cos, jnp.ones((seq, rest), F32)], axis=1)
    sin_t = jnp.concatenate([-sin, sin, jnp.zeros((seq, rest), F32)], axis=1)
    return cos_t, sin_t


def kernel(x, ffn_norm, ffn_w_gate, ffn_w_up, ffn_w_down, mix_norm, w_in, sg_norm, sg_w, sg_b, q_norm, k_norm, gla_w_gate, gla_b_gate, gla_out_norm, w_out):
    batch, seq, d = x.shape
    depth = w_in.shape[0]
    n = batch * seq
    sg_width = sg_norm.shape[-1]
    gla_qk = gla_w_gate.shape[-1]
    gla_width = (gla_qk // GLA_DK) * HEAD_DIM
    att_width = d - sg_width - gla_width
    n_in = w_in.shape[-1]
    assert n_in == 2 * sg_width + 3 * att_width + 2 * gla_qk + 2 * gla_width + GLA_GATE_RANK
    widths = (2 * sg_width, 3 * att_width, 2 * gla_qk + 2 * gla_width + LANES)

    wg, wu, wd = ffn_w_gate.astype(BF16), ffn_w_up.astype(BF16), ffn_w_down.astype(BF16)
    w_in_p = jnp.pad(w_in, ((0, 0), (0, 0), (0, LANES - GLA_GATE_RANK))).astype(BF16)
    w_out_b = w_out.astype(BF16)
    gla_wg = jnp.pad(gla_w_gate, ((0, 0), (0, LANES - GLA_GATE_RANK), (0, 0))).astype(BF16)
    ffn_g = ffn_norm.reshape(depth, 2, 1, d)
    mix_g = mix_norm.reshape(depth, 1, d)
    sg_g = sg_norm.reshape(depth, 1, sg_width)
    sg_bt = jnp.swapaxes(sg_b, 1, 2)
    q_g = q_norm.reshape(depth, 1, HEAD_DIM)
    k_g = k_norm.reshape(depth, 1, HEAD_DIM)
    gla_bg = gla_b_gate.reshape(depth, 1, gla_qk)
    gla_og = gla_out_norm.reshape(depth, 1, HEAD_DIM)
    cos_t, sin_t = _rope_tables(seq)

    h = x.reshape(n, d)
    for l in range(depth):
        h = _ffn(h, ffn_g, wg, wu, wd, l, 0)
        zs, za, zg = _inproj(h, mix_g, w_in_p, cos_t, sin_t, q_g, k_g, l, widths)
        ya = _sgu(zs, sg_g, sg_w, sg_bt, l)
        yb = _attention(za, batch)
        yc = _gla(zg, gla_wg, gla_bg, gla_og, l, batch)
        h = _outproj(h, (ya, yb, yc), w_out_b, l)
        h = _ffn(h, ffn_g, wg, wu, wd, l, 1)
    return h.reshape(batch, seq, d)
```

```python
import functools

import jax
import jax.numpy as jnp
from jax import lax
from jax.experimental import pallas as pl
from jax.experimental.pallas import tpu as pltpu

F32 = jnp.float32
BF16 = jnp.bfloat16

EPS = 1e-6
HEAD_DIM = 128
LANES = 128
SG_CHUNK = 128
ATT_DILATIONS = (1, 4, 16)
ATT_STEPS = 128
ATT_RESIDUES = 16
ROPE_THETA = 500000.0
ROPE_DIMS = HEAD_DIM // 4
GLA_DK = 64
GLA_CHUNK = 64
GLA_GATE_RANK = 16
GLA_GATE_NORMALIZER = 16.0
MASK_VALUE = -1e30

VMEM_LIMIT_BYTES = 56 * 1024 * 1024

FFN_TM, FFN_TF = 1024, 512
PROJ_TM = 512
GLA_ROWS = 1024
ATT_PREP_ROWS = 256
ATT_GROUP = 8
LOG2_E = 1.4426950408889634


def _params(*semantics):
    return pltpu.CompilerParams(dimension_semantics=semantics, vmem_limit_bytes=VMEM_LIMIT_BYTES)


def _rms_norm(x, g):
    return x * lax.rsqrt(jnp.mean(x * x, axis=-1, keepdims=True) + EPS) * g


def _dot(a, b):
    return jnp.dot(a, b, preferred_element_type=F32)


def _dot_nt(a, b):
    return lax.dot_general(a, b, (((1,), (1,)), ((), ())), preferred_element_type=F32)


def _dot_tn(a, b):
    return lax.dot_general(a, b, (((0,), (0,)), ((), ())), preferred_element_type=F32)


def _ffn_kernel(x_ref, g_ref, wg_ref, wu_ref, wd_ref, o_ref, h_ref):
    @pl.when(pl.program_id(1) == 0)
    def _():
        x = x_ref[...]
        h_ref[...] = _rms_norm(x, g_ref[...]).astype(BF16)
        o_ref[...] = x

    h = h_ref[...]
    gate = _dot(h, wg_ref[...])
    up = _dot(h, wu_ref[...])
    act = (gate * jax.nn.sigmoid(gate)) * (0.5 * up)
    o_ref[...] += _dot(act.astype(BF16), wd_ref[...])


def _ffn(x, norm_g, w_gate, w_up, w_down, l, j):
    n, d = x.shape
    f = w_gate.shape[-1]
    tm, tf = min(FFN_TM, n), FFN_TF
    return pl.pallas_call(
        _ffn_kernel,
        grid=(n // tm, f // tf),
        in_specs=[
            pl.BlockSpec((tm, d), lambda i, k: (i, 0)),
            pl.BlockSpec((None, None, 1, d), lambda i, k: (l, j, 0, 0)),
            pl.BlockSpec((None, None, d, tf), lambda i, k: (l, j, 0, k)),
            pl.BlockSpec((None, None, d, tf), lambda i, k: (l, j, 0, k)),
            pl.BlockSpec((None, None, tf, d), lambda i, k: (l, j, k, 0)),
        ],
        out_specs=pl.BlockSpec((tm, d), lambda i, k: (i, 0)),
        out_shape=jax.ShapeDtypeStruct((n, d), F32),
        scratch_shapes=[pltpu.VMEM((tm, d), BF16)],
        compiler_params=_params("parallel", "arbitrary"),
        name="ffn",
    )(x, norm_g, w_gate, w_up, w_down)


def _inproj_kernel(x_ref, g_ref, w_ref, cos_ref, sin_ref, qg_ref, kg_ref, sgn_ref, sgw_ref, sgbt_ref,
                   ya_ref, za_ref, zg_ref):
    h = _rms_norm(x_ref[...], g_ref[...]).astype(BF16)
    rows, sgw = ya_ref.shape
    c0 = 2 * sgw
    c1 = c0 + za_ref.shape[1]
    zs = _dot(h, w_ref[:, :c0])
    za = _dot(h, w_ref[:, c0:c1])
    zg_ref[...] = _dot(h, w_ref[:, c1:]).astype(BF16)

    heads = za_ref.shape[1] // (3 * HEAD_DIM)
    lane = lax.broadcasted_iota(jnp.int32, (1, HEAD_DIM), 1)
    half = ROPE_DIMS // 2
    cos = cos_ref[...]
    sin = sin_ref[...]
    for j in range(2 * heads):
        cols = slice(j * HEAD_DIM, (j + 1) * HEAD_DIM)
        gain, scale = (qg_ref, HEAD_DIM ** -0.5 * LOG2_E) if j < heads else (kg_ref, None)
        t = _rms_norm(za[:, cols], gain[...])
        partner = jnp.where(lane < half, pltpu.roll(t, HEAD_DIM - half, 1), pltpu.roll(t, half, 1))
        t = t * cos + partner * sin
        if scale is not None:
            t = t * scale
        za_ref[:, cols] = t.astype(BF16)
    za_ref[:, 2 * heads * HEAD_DIM:] = za[:, 2 * heads * HEAD_DIM:].astype(BF16)

    t = SG_CHUNK
    chunks = rows // t
    causal = lax.broadcasted_iota(jnp.int32, (t, t), 1) <= lax.broadcasted_iota(jnp.int32, (t, t), 0)
    for g in range(sgw // HEAD_DIM):
        cu = slice(g * HEAD_DIM, (g + 1) * HEAD_DIM)
        cv = slice(sgw + g * HEAD_DIM, sgw + (g + 1) * HEAD_DIM)
        v = _rms_norm(jax.nn.gelu(zs[:, cv]), sgn_ref[:, cu]).astype(BF16)
        v = jnp.concatenate([v[c * t:(c + 1) * t] for c in range(chunks)], axis=1)
        sv = _dot(jnp.where(causal, sgw_ref[g], 0.0).astype(BF16), v) + sgbt_ref[:, g:g + 1]
        sv = jnp.concatenate([sv[:, c * HEAD_DIM:(c + 1) * HEAD_DIM] for c in range(chunks)], axis=0)
        ya_ref[:, cu] = (jax.nn.gelu(zs[:, cu]) * sv).astype(BF16)


def _inproj(x, norm_g, w_in, cos_t, sin_t, q_gain, k_gain, sg_norm, sg_w, sg_bt, l, widths):
    n, d = x.shape
    c = w_in.shape[-1]
    seq = cos_t.shape[0]
    sgw = sg_norm.shape[-1]
    groups = sgw // HEAD_DIM
    tm = min(PROJ_TM, seq)
    assert seq % tm == 0 and tm % SG_CHUNK == 0

    def layer(*block):
        return pl.BlockSpec((None,) + block, lambda i: (l,) + (0,) * len(block))

    return pl.pallas_call(
        _inproj_kernel,
        grid=(n // tm,),
        in_specs=[
            pl.BlockSpec((tm, d), lambda i: (i, 0)),
            layer(1, d),
            pl.BlockSpec((None, d, c), lambda i: (l, 0, 0), pipeline_mode=pl.Buffered(1)),
            pl.BlockSpec((tm, HEAD_DIM), lambda i: (i % (seq // tm), 0)),
            pl.BlockSpec((tm, HEAD_DIM), lambda i: (i % (seq // tm), 0)),
            layer(1, HEAD_DIM),
            layer(1, HEAD_DIM),
            layer(1, sgw),
            layer(groups, SG_CHUNK, SG_CHUNK),
            layer(SG_CHUNK, groups),
        ],
        out_specs=[pl.BlockSpec((tm, w), lambda i: (i, 0)) for w in widths],
        out_shape=[jax.ShapeDtypeStruct((n, w), BF16) for w in widths],
        compiler_params=_params("parallel"),
        name="in_proj",
    )(x, norm_g, w_in, cos_t, sin_t, q_gain, k_gain, sg_norm, sg_w, sg_bt)


def _att_kernel(qn, kn, v_ref, o_ref, qd, kd, vd, acc, m_s, l_s, s_buf, mb_buf):
    seq = qn.shape[0]
    w = ATT_STEPS
    res = ATT_RESIDUES
    per = seq // res
    pitch = acc.shape[0] // res
    pr = min(ATT_PREP_ROWS, seq)

    def stage(src, dst):
        def body(c, carry):
            t = src[pl.ds(pl.multiple_of(c * pr, pr), pr), :].astype(F32)
            for j in range(pr // res):
                acc[pl.ds(c * (pr // res) + j, res, stride=pitch), :] = t[j * res:(j + 1) * res, :]
            return carry

        lax.fori_loop(0, seq // pr, body, 0, unroll=2)

        def repack(r, carry):
            dst[r] = acc[pl.ds(pl.multiple_of(r * pitch, 8), per), :].astype(BF16)
            return carry

        lax.fori_loop(0, res, repack, 0)

    stage(qn, qd)
    stage(kn, kd)
    stage(v_ref, vd)

    row = lax.broadcasted_iota(jnp.int32, (w, w), 0)
    col = lax.broadcasted_iota(jnp.int32, (w, w), 1)
    group = s_buf.shape[0]

    def run_branch(load_qk, load_v, load_state, store, order_diff):
        def scores(g):
            for u in range(group):
                q, k2, has_prev = load_qk(g * group + u)
                s = _dot_nt(q, k2)
                prev_min = jnp.where(has_prev, 0, 2 * w)
                s = jnp.concatenate([jnp.where(order_diff >= prev_min, s[:, :w], MASK_VALUE),
                                     jnp.where(order_diff <= 0, s[:, w:], MASK_VALUE)], axis=1)
                s_buf[u] = s
                mb_buf[u] = jnp.broadcast_to(jnp.max(s, axis=1, keepdims=True), (w, HEAD_DIM))

        def consume(g):
            for u in range(group):
                i = g * group + u
                old = load_state(i)
                m_blk = mb_buf[u]
                m_new = m_blk if old is None else jnp.maximum(old[0], m_blk)
                p = jnp.exp2(s_buf[u] - jnp.concatenate([m_new, m_new], axis=1))
                p_sum = jnp.sum(p, axis=1, keepdims=True)
                pv = _dot(p.astype(BF16), load_v(i))
                if old is None:
                    store(i, m_new, jnp.broadcast_to(p_sum, (w, HEAD_DIM)), pv)
                else:
                    alpha = jnp.exp2(old[0] - m_new)
                    store(i, m_new, alpha * old[1] + p_sum, alpha * old[2] + pv)

        def body(g, carry):
            consume(g - 1)
            scores(g)
            return carry

        groups = seq // (w * group)
        scores(0)
        lax.fori_loop(1, groups, body, 0)
        consume(groups - 1)

    for d in ATT_DILATIONS[:0:-1]:
        first = d == ATT_DILATIONS[-1]
        pieces = res // d
        plen = w // pieces
        nb = per // plen
        order = lambda x: pieces * (x % plen) + x // plen

        def gather(ref, r, at, d=d, pieces=pieces, plen=plen):
            return jnp.concatenate([ref[c * d + r, pl.ds(at, plen), :] for c in range(pieces)], axis=0)

        def starts(i, nb=nb, plen=plen):
            n = i % nb
            return (i // nb, n, pl.multiple_of(n * plen, plen),
                    pl.multiple_of(jnp.where(n > 0, n - 1, n) * plen, plen))

        def state_rows(r, c, start, d=d, plen=plen):
            return pl.ds(pl.multiple_of((c * d + r) * pitch + start, 8), plen)

        def load_qk(i, gather=gather, starts=starts):
            r, n, start, pstart = starts(i)
            return gather(qd, r, start), jnp.concatenate([gather(kd, r, pstart), gather(kd, r, start)], axis=0), n > 0

        def load_v(i, gather=gather, starts=starts):
            r, _, start, pstart = starts(i)
            return jnp.concatenate([gather(vd, r, pstart), gather(vd, r, start)], axis=0)

        def load_state(i, first=first, pieces=pieces, starts=starts, state_rows=state_rows):
            if first:
                return None
            r, _, start, _ = starts(i)
            return tuple(jnp.concatenate([ref[state_rows(r, c, start), :] for c in range(pieces)], axis=0)
                         for ref in (m_s, l_s, acc))

        def store(i, *new, pieces=pieces, plen=plen, starts=starts, state_rows=state_rows):
            r, _, start, _ = starts(i)
            for ref, val in zip((m_s, l_s, acc), new):
                for c in range(pieces):
                    ref[state_rows(r, c, start), :] = val[c * plen:(c + 1) * plen, :]

        run_branch(load_qk, load_v, load_state, store, order(col) - order(row))

    def nat_starts(n):
        return pl.multiple_of(n * w, w), pl.multiple_of(jnp.where(n > 0, n - 1, n) * w, w)

    def nat_qk(n):
        start, pstart = nat_starts(n)
        return (qn[pl.ds(start, w), :],
                jnp.concatenate([kn[pl.ds(pstart, w), :], kn[pl.ds(start, w), :]], axis=0), n > 0)

    def nat_v(n):
        start, pstart = nat_starts(n)
        return jnp.concatenate([v_ref[pl.ds(pstart, w), :], v_ref[pl.ds(start, w), :]], axis=0)

    def nat_state(n):
        return tuple(jnp.concatenate([ref[pl.ds(n * (w // res) + j, res, stride=pitch), :]
                                      for j in range(w // res)], axis=0) for ref in (m_s, l_s, acc))

    def emit(n, m_new, l_new, acc_new):
        o_ref[pl.ds(nat_starts(n)[0], w), :] = (acc_new / l_new).astype(BF16)

    run_branch(nat_qk, nat_v, nat_state, emit, col - row)


def _attention(za, batch):
    n, w3 = za.shape
    seq = n // batch
    width = w3 // 3
    heads = width // HEAD_DIM
    assert seq % (ATT_STEPS * ATT_RESIDUES) == 0
    per = seq // ATT_RESIDUES
    pitch = per + 8
    za3 = za.reshape(batch, seq, w3)
    blk = (None, seq, HEAD_DIM)
    out = pl.pallas_call(
        _att_kernel,
        grid=(batch, heads),
        in_specs=[
            pl.BlockSpec(blk, lambda b, h: (b, 0, h)),
            pl.BlockSpec(blk, lambda b, h: (b, 0, heads + h)),
            pl.BlockSpec(blk, lambda b, h: (b, 0, 2 * heads + h)),
        ],
        out_specs=pl.BlockSpec(blk, lambda b, h: (b, 0, h)),
        out_shape=jax.ShapeDtypeStruct((batch, seq, width), BF16),
        scratch_shapes=[pltpu.VMEM((ATT_RESIDUES, per, HEAD_DIM), BF16) for _ in range(3)]
        + [pltpu.VMEM((ATT_RESIDUES * pitch, HEAD_DIM), F32) for _ in range(3)]
        + [pltpu.VMEM((ATT_GROUP, ATT_STEPS, 2 * ATT_STEPS), F32), pltpu.VMEM((ATT_GROUP, ATT_STEPS, HEAD_DIM), F32)],
        compiler_params=_params("parallel", "parallel"),
        name="dilated_attention",
    )(za3, za3, za3)
    return out.reshape(n, width)


def _gla_kernel(q_ref, k_ref, v_ref, g_ref, r_ref, wg_ref, bg_ref, og_ref, o_ref, st_ref, o_acc):
    rows = q_ref.shape[0]
    ck = GLA_CHUNK
    dv = HEAD_DIM

    @pl.when(pl.program_id(2) == 0)
    def _():
        st_ref[...] = jnp.zeros_like(st_ref)

    x = _dot(r_ref[...], wg_ref[...]) + bg_ref[...]
    log_a = (jnp.minimum(x, 0.0) - jnp.log1p(jnp.exp(-jnp.abs(x)))) / GLA_GATE_NORMALIZER
    hi = log_a.astype(BF16)
    lo = (log_a - hi.astype(F32)).astype(BF16)
    grp = 4 * ck
    r_i = lax.broadcasted_iota(jnp.int32, (grp, grp), 0)
    c_i = lax.broadcasted_iota(jnp.int32, (grp, grp), 1)
    same = (r_i // ck) == (c_i // ck)
    tri = jnp.where(same & (c_i <= r_i), 1.0, 0.0).astype(BF16)
    ones = jnp.where(same, 1.0, 0.0).astype(BF16)
    b_parts, bl_parts = [], []
    for s in range(rows // grp):
        rs = slice(s * grp, (s + 1) * grp)
        b_parts.append(_dot(tri, hi[rs]) + _dot(tri, lo[rs]))
        bl_parts.append(_dot(ones, hi[rs]) + _dot(ones, lo[rs]))
    b = jnp.concatenate(b_parts, axis=0)
    b_last = jnp.concatenate(bl_parts, axis=0)

    q = q_ref[...].astype(F32) * GLA_DK ** -0.5
    k = k_ref[...].astype(F32)
    q_dec = (q * jnp.exp(b)).astype(BF16)
    k_inv = (k * jnp.exp(-b)).astype(BF16)
    k_dec = (k * jnp.exp(b_last - b)).astype(BF16)
    decay = jnp.exp(b_last)

    lane = lax.broadcasted_iota(jnp.int32, (1, 2 * GLA_DK), 1)
    head0 = lane < GLA_DK
    ar = lax.broadcasted_iota(jnp.int32, (ck, 2 * ck), 0)
    ac = lax.broadcasted_iota(jnp.int32, (ck, 2 * ck), 1)
    causal = (ac % ck) <= ar
    sr = lax.broadcasted_iota(jnp.int32, (2 * dv, 2 * GLA_DK), 0)
    sc = lax.broadcasted_iota(jnp.int32, (2 * dv, 2 * GLA_DK), 1)
    own = (sr < dv) == (sc < GLA_DK)
    zeros_v = jnp.zeros((ck, dv), BF16)

    state = st_ref[...]
    for c in range(rows // ck):
        rs = slice(c * ck, (c + 1) * ck)
        qd = q_dec[rs]
        ki = k_inv[rs]
        ki2 = jnp.concatenate([jnp.where(head0, ki, 0), jnp.where(head0, 0, ki)], axis=0)
        a = jnp.where(causal, _dot_nt(qd, ki2), 0.0)
        vp = v_ref[rs, :]
        v_bd = jnp.concatenate(
            [jnp.concatenate([vp[:, :dv], zeros_v], axis=1),
             jnp.concatenate([zeros_v, vp[:, dv:]], axis=1)], axis=0)
        o_acc[rs, :] = _dot(a.astype(BF16), v_bd) + _dot_nt(qd, state.astype(BF16))
        u = _dot_tn(vp, k_dec[rs])
        state = state * decay[c * ck:c * ck + 1, :] + jnp.where(own, u, 0.0)
    st_ref[...] = state

    for h in range(2):
        cs = slice(h * dv, (h + 1) * dv)
        g = g_ref[:, cs].astype(F32)
        o = _rms_norm(o_acc[:, cs], og_ref[...]) * (g * jax.nn.sigmoid(g))
        o_ref[:, cs] = o.astype(BF16)


def _gla(zg, w_gate, b_gate, out_gain, l, batch):
    n, _ = zg.shape
    seq = n // batch
    dkp = 2 * GLA_DK
    dvp = 2 * HEAD_DIM
    pairs = w_gate.shape[-1] // dkp
    width = pairs * dvp
    rows = min(GLA_ROWS, seq)
    zg3 = zg.reshape(batch, seq, zg.shape[1])
    qk_blocks = 2 * pairs * dkp // dvp
    out = pl.pallas_call(
        _gla_kernel,
        grid=(batch, pairs, seq // rows),
        in_specs=[
            pl.BlockSpec((None, rows, dkp), lambda b, p, t: (b, t, p)),
            pl.BlockSpec((None, rows, dkp), lambda b, p, t: (b, t, pairs + p)),
            pl.BlockSpec((None, rows, dvp), lambda b, p, t: (b, t, qk_blocks + p)),
            pl.BlockSpec((None, rows, dvp), lambda b, p, t: (b, t, qk_blocks + pairs + p)),
            pl.BlockSpec((None, rows, LANES), lambda b, p, t: (b, t, (2 * pairs * dkp + 2 * width) // LANES)),
            pl.BlockSpec((None, LANES, dkp), lambda b, p, t: (l, 0, p)),
            pl.BlockSpec((None, 1, dkp), lambda b, p, t: (l, 0, p)),
            pl.BlockSpec((None, 1, HEAD_DIM), lambda b, p, t: (l, 0, 0)),
        ],
        out_specs=pl.BlockSpec((None, rows, dvp), lambda b, p, t: (b, t, p)),
        out_shape=jax.ShapeDtypeStruct((batch, seq, width), BF16),
        scratch_shapes=[pltpu.VMEM((dvp, dkp), F32), pltpu.VMEM((rows, dvp), F32)],
        compiler_params=_params("parallel", "parallel", "arbitrary"),
        name="gated_linear_attention",
    )(zg3, zg3, zg3, zg3, zg3, w_gate, b_gate, out_gain)
    return out.reshape(n, width)


def _outproj_kernel(x_ref, ya_ref, yb_ref, yc_ref, w_ref, o_ref):
    acc = x_ref[...]
    c = 0
    for y_ref in (ya_ref, yb_ref, yc_ref):
        wdt = y_ref.shape[1]
        acc = acc + _dot(y_ref[...], w_ref[c:c + wdt, :])
        c += wdt
    o_ref[...] = acc


def _outproj(x, ys, w_out, l):
    n, d = x.shape
    tm = min(PROJ_TM, n)
    return pl.pallas_call(
        _outproj_kernel,
        grid=(n // tm,),
        in_specs=[pl.BlockSpec((tm, d), lambda i: (i, 0))]
        + [pl.BlockSpec((tm, y.shape[1]), lambda i: (i, 0)) for y in ys]
        + [pl.BlockSpec((None, w_out.shape[1], d), lambda i: (l, 0, 0), pipeline_mode=pl.Buffered(1))],
        out_specs=pl.BlockSpec((tm, d), lambda i: (i, 0)),
        out_shape=jax.ShapeDtypeStruct((n, d), F32),
        compiler_params=_params("parallel"),
        name="out_proj",
    )(x, *ys, w_out)


def _rope_tables(seq):
    pos = jnp.arange(seq, dtype=F32)
    inv_freq = ROPE_THETA ** (-jnp.arange(0, ROPE_DIMS, 2, dtype=F32) / ROPE_DIMS)
    ang = pos[:, None] * inv_freq[None, :]
    cos, sin = jnp.cos(ang), jnp.sin(ang)
    rest = HEAD_DIM - ROPE_DIMS
    cos_t = jnp.concatenate([cos, cos, jnp.ones((seq, rest), F32)], axis=1)
    sin_t = jnp.concatenate([-sin, sin, jnp.zeros((seq, rest), F32)], axis=1)
    return cos_t, sin_t


def kernel(x, ffn_norm, ffn_w_gate, ffn_w_up, ffn_w_down, mix_norm, w_in, sg_norm, sg_w, sg_b, q_norm, k_norm, gla_w_gate, gla_b_gate, gla_out_norm, w_out):
    batch, seq, d = x.shape
    depth = w_in.shape[0]
    n = batch * seq
    sg_width = sg_norm.shape[-1]
    gla_qk = gla_w_gate.shape[-1]
    gla_width = (gla_qk // GLA_DK) * HEAD_DIM
    att_width = d - sg_width - gla_width
    n_in = w_in.shape[-1]
    assert n_in == 2 * sg_width + 3 * att_width + 2 * gla_qk + 2 * gla_width + GLA_GATE_RANK
    widths = (sg_width, 3 * att_width, 2 * gla_qk + 2 * gla_width + LANES)

    wg, wu, wd = ffn_w_gate.astype(BF16), ffn_w_up.astype(BF16), ffn_w_down.astype(BF16)
    w_in_p = jnp.pad(w_in, ((0, 0), (0, 0), (0, LANES - GLA_GATE_RANK))).astype(BF16)
    w_out_b = w_out.astype(BF16)
    gla_wg = jnp.pad(gla_w_gate, ((0, 0), (0, LANES - GLA_GATE_RANK), (0, 0))).astype(BF16)
    ffn_g = ffn_norm.reshape(depth, 2, 1, d)
    mix_g = mix_norm.reshape(depth, 1, d)
    sg_g = sg_norm.reshape(depth, 1, sg_width)
    sg_bt = jnp.swapaxes(sg_b, 1, 2)
    q_g = q_norm.reshape(depth, 1, HEAD_DIM)
    k_g = k_norm.reshape(depth, 1, HEAD_DIM)
    gla_bg = gla_b_gate.reshape(depth, 1, gla_qk)
    gla_og = gla_out_norm.reshape(depth, 1, HEAD_DIM)
    cos_t, sin_t = _rope_tables(seq)

    h = x.reshape(n, d)
    for l in range(depth):
        h = _ffn(h, ffn_g, wg, wu, wd, l, 0)
        ya, za, zg = _inproj(h, mix_g, w_in_p, cos_t, sin_t, q_g, k_g, sg_g, sg_w, sg_bt, l, widths)
        yb = _attention(za, batch)
        yc = _gla(zg, gla_wg, gla_bg, gla_og, l, batch)
        h = _outproj(h, (ya, yb, yc), w_out_b, l)
        h = _ffn(h, ffn_g, wg, wu, wd, l, 1)
    return h.reshape(batch, seq, d)
```

```python
import functools

import jax
import jax.numpy as jnp
from jax import lax
from jax.experimental import pallas as pl
from jax.experimental.pallas import tpu as pltpu

F32 = jnp.float32
BF16 = jnp.bfloat16

EPS = 1e-6
HEAD_DIM = 128
LANES = 128
SG_CHUNK = 128
ATT_DILATIONS = (1, 4, 16)
ATT_STEPS = 128
ATT_RESIDUES = 16
ROPE_THETA = 500000.0
ROPE_DIMS = HEAD_DIM // 4
GLA_DK = 64
GLA_CHUNK = 64
GLA_GATE_RANK = 16
GLA_GATE_NORMALIZER = 16.0
MASK_VALUE = -1e30

VMEM_LIMIT_BYTES = 56 * 1024 * 1024

FFN_TM, FFN_TF = 1024, 512
PROJ_TM = 512
GLA_ROWS = 1024
ATT_PREP_ROWS = 256
ATT_GROUP = 8
LOG2_E = 1.4426950408889634


def _params(*semantics):
    return pltpu.CompilerParams(dimension_semantics=semantics, vmem_limit_bytes=VMEM_LIMIT_BYTES)


def _rms_norm(x, g):
    return x * lax.rsqrt(jnp.mean(x * x, axis=-1, keepdims=True) + EPS) * g


def _dot(a, b):
    return jnp.dot(a, b, preferred_element_type=F32)


def _dot_nt(a, b):
    return lax.dot_general(a, b, (((1,), (1,)), ((), ())), preferred_element_type=F32)


def _dot_tn(a, b):
    return lax.dot_general(a, b, (((0,), (0,)), ((), ())), preferred_element_type=F32)


def _ffn_kernel(x_ref, g_ref, wg_ref, wu_ref, wd_ref, o_ref, h_ref):
    @pl.when(pl.program_id(1) == 0)
    def _():
        x = x_ref[...]
        h_ref[...] = _rms_norm(x, g_ref[...]).astype(BF16)
        o_ref[...] = x

    h = h_ref[...]
    gate = _dot(h, wg_ref[...])
    up = _dot(h, wu_ref[...])
    act = (gate * jax.nn.sigmoid(gate)) * (0.5 * up)
    o_ref[...] += _dot(act.astype(BF16), wd_ref[...])


def _ffn_column_blocks(w):
    *lead, d, f = w.shape
    w = w.astype(BF16).reshape(*lead, d, f // FFN_TF, FFN_TF)
    return jnp.swapaxes(w, -3, -2)


def _ffn(x, norm_g, w_gate, w_up, w_down, l, j):
    n, d = x.shape
    tm = min(FFN_TM, n)
    steps, _, tf = w_gate.shape[-3:]
    return pl.pallas_call(
        _ffn_kernel,
        grid=(n // tm, steps),
        in_specs=[
            pl.BlockSpec((tm, d), lambda i, k: (i, 0)),
            pl.BlockSpec((None, None, 1, d), lambda i, k: (l, j, 0, 0)),
            pl.BlockSpec((None, None, None, d, tf), lambda i, k: (l, j, k, 0, 0)),
            pl.BlockSpec((None, None, None, d, tf), lambda i, k: (l, j, k, 0, 0)),
            pl.BlockSpec((None, None, tf, d), lambda i, k: (l, j, k, 0)),
        ],
        out_specs=pl.BlockSpec((tm, d), lambda i, k: (i, 0)),
        out_shape=jax.ShapeDtypeStruct((n, d), F32),
        scratch_shapes=[pltpu.VMEM((tm, d), BF16)],
        compiler_params=_params("parallel", "arbitrary"),
        name="ffn",
    )(x, norm_g, w_gate, w_up, w_down)


def _inproj_kernel(x_ref, g_ref, w_ref, cos_ref, sin_ref, qg_ref, kg_ref, sgn_ref, sgw_ref, sgbt_ref,
                   ya_ref, za_ref, zg_ref):
    h = _rms_norm(x_ref[...], g_ref[...]).astype(BF16)
    rows, sgw = ya_ref.shape
    c0 = 2 * sgw
    c1 = c0 + za_ref.shape[1]
    zs = _dot(h, w_ref[:, :c0])
    za = _dot(h, w_ref[:, c0:c1])
    zg_ref[...] = _dot(h, w_ref[:, c1:]).astype(BF16)

    heads = za_ref.shape[1] // (3 * HEAD_DIM)
    lane = lax.broadcasted_iota(jnp.int32, (1, HEAD_DIM), 1)
    half = ROPE_DIMS // 2
    cos = cos_ref[...]
    sin = sin_ref[...]
    for j in range(2 * heads):
        cols = slice(j * HEAD_DIM, (j + 1) * HEAD_DIM)
        gain, scale = (qg_ref, HEAD_DIM ** -0.5 * LOG2_E) if j < heads else (kg_ref, None)
        t = _rms_norm(za[:, cols], gain[...])
        partner = jnp.where(lane < half, pltpu.roll(t, HEAD_DIM - half, 1), pltpu.roll(t, half, 1))
        t = t * cos + partner * sin
        if scale is not None:
            t = t * scale
        za_ref[:, cols] = t.astype(BF16)
    za_ref[:, 2 * heads * HEAD_DIM:] = za[:, 2 * heads * HEAD_DIM:].astype(BF16)

    t = SG_CHUNK
    chunks = rows // t
    causal = lax.broadcasted_iota(jnp.int32, (t, t), 1) <= lax.broadcasted_iota(jnp.int32, (t, t), 0)
    for g in range(sgw // HEAD_DIM):
        cu = slice(g * HEAD_DIM, (g + 1) * HEAD_DIM)
        cv = slice(sgw + g * HEAD_DIM, sgw + (g + 1) * HEAD_DIM)
        v = _rms_norm(jax.nn.gelu(zs[:, cv]), sgn_ref[:, cu]).astype(BF16)
        v = jnp.concatenate([v[c * t:(c + 1) * t] for c in range(chunks)], axis=1)
        sv = _dot(jnp.where(causal, sgw_ref[g], 0.0).astype(BF16), v) + sgbt_ref[:, g:g + 1]
        sv = jnp.concatenate([sv[:, c * HEAD_DIM:(c + 1) * HEAD_DIM] for c in range(chunks)], axis=0)
        ya_ref[:, cu] = (jax.nn.gelu(zs[:, cu]) * sv).astype(BF16)


def _inproj(x, norm_g, w_in, cos_t, sin_t, q_gain, k_gain, sg_norm, sg_w, sg_bt, l, widths):
    n, d = x.shape
    c = w_in.shape[-1]
    seq = cos_t.shape[0]
    sgw = sg_norm.shape[-1]
    groups = sgw // HEAD_DIM
    tm = min(PROJ_TM, seq)
    assert seq % tm == 0 and tm % SG_CHUNK == 0

    def layer(*block):
        return pl.BlockSpec((None,) + block, lambda i: (l,) + (0,) * len(block))

    return pl.pallas_call(
        _inproj_kernel,
        grid=(n // tm,),
        in_specs=[
            pl.BlockSpec((tm, d), lambda i: (i, 0)),
            layer(1, d),
            pl.BlockSpec((None, d, c), lambda i: (l, 0, 0), pipeline_mode=pl.Buffered(1)),
            pl.BlockSpec((tm, HEAD_DIM), lambda i: (i % (seq // tm), 0)),
            pl.BlockSpec((tm, HEAD_DIM), lambda i: (i % (seq // tm), 0)),
            layer(1, HEAD_DIM),
            layer(1, HEAD_DIM),
            layer(1, sgw),
            layer(groups, SG_CHUNK, SG_CHUNK),
            layer(SG_CHUNK, groups),
        ],
        out_specs=[pl.BlockSpec((tm, w), lambda i: (i, 0)) for w in widths],
        out_shape=[jax.ShapeDtypeStruct((n, w), BF16) for w in widths],
        compiler_params=_params("parallel"),
        name="in_proj",
    )(x, norm_g, w_in, cos_t, sin_t, q_gain, k_gain, sg_norm, sg_w, sg_bt)


def _att_kernel(qn, kn, v_ref, o_ref, qd, kd, vd, acc, m_s, l_s, s_buf, mb_buf):
    seq = qn.shape[0]
    w = ATT_STEPS
    res = ATT_RESIDUES
    per = seq // res
    pitch = acc.shape[0] // res
    pr = min(ATT_PREP_ROWS, seq)

    def stage(src, dst):
        def body(c, carry):
            t = src[pl.ds(pl.multiple_of(c * pr, pr), pr), :].astype(F32)
            for j in range(pr // res):
                acc[pl.ds(c * (pr // res) + j, res, stride=pitch), :] = t[j * res:(j + 1) * res, :]
            return carry

        lax.fori_loop(0, seq // pr, body, 0, unroll=2)

        def repack(r, carry):
            dst[r] = acc[pl.ds(pl.multiple_of(r * pitch, 8), per), :].astype(BF16)
            return carry

        lax.fori_loop(0, res, repack, 0)

    stage(qn, qd)
    stage(kn, kd)
    stage(v_ref, vd)

    row = lax.broadcasted_iota(jnp.int32, (w, w), 0)
    col = lax.broadcasted_iota(jnp.int32, (w, w), 1)
    group = s_buf.shape[0]

    def run_branch(load_qk, load_v, load_state, store, order_diff):
        def scores(g):
            for u in range(group):
                q, k2, has_prev = load_qk(g * group + u)
                s = _dot_nt(q, k2)
                prev_min = jnp.where(has_prev, 0, 2 * w)
                s = jnp.concatenate([jnp.where(order_diff >= prev_min, s[:, :w], MASK_VALUE),
                                     jnp.where(order_diff <= 0, s[:, w:], MASK_VALUE)], axis=1)
                s_buf[u] = s
                mb_buf[u] = jnp.broadcast_to(jnp.max(s, axis=1, keepdims=True), (w, HEAD_DIM))

        def consume(g):
            for u in range(group):
                i = g * group + u
                old = load_state(i)
                m_blk = mb_buf[u]
                m_new = m_blk if old is None else jnp.maximum(old[0], m_blk)
                p = jnp.exp2(s_buf[u] - jnp.concatenate([m_new, m_new], axis=1))
                p_sum = jnp.sum(p, axis=1, keepdims=True)
                pv = _dot(p.astype(BF16), load_v(i))
                if old is None:
                    store(i, m_new, jnp.broadcast_to(p_sum, (w, HEAD_DIM)), pv)
                else:
                    alpha = jnp.exp2(old[0] - m_new)
                    store(i, m_new, alpha * old[1] + p_sum, alpha * old[2] + pv)

        def body(g, carry):
            consume(g - 1)
            scores(g)
            return carry

        groups = seq // (w * group)
        scores(0)
        lax.fori_loop(1, groups, body, 0)
        consume(groups - 1)

    for d in ATT_DILATIONS[:0:-1]:
        first = d == ATT_DILATIONS[-1]
        pieces = res // d
        plen = w // pieces
        nb = per // plen
        order = lambda x: pieces * (x % plen) + x // plen

        def gather(ref, r, at, d=d, pieces=pieces, plen=plen):
            return jnp.concatenate([ref[c * d + r, pl.ds(at, plen), :] for c in range(pieces)], axis=0)

        def starts(i, nb=nb, plen=plen):
            n = i % nb
            return (i // nb, n, pl.multiple_of(n * plen, plen),
                    pl.multiple_of(jnp.where(n > 0, n - 1, n) * plen, plen))

        def state_rows(r, c, start, d=d, plen=plen):
            return pl.ds(pl.multiple_of((c * d + r) * pitch + start, 8), plen)

        def load_qk(i, gather=gather, starts=starts):
            r, n, start, pstart = starts(i)
            return gather(qd, r, start), jnp.concatenate([gather(kd, r, pstart), gather(kd, r, start)], axis=0), n > 0

        def load_v(i, gather=gather, starts=starts):
            r, _, start, pstart = starts(i)
            return jnp.concatenate([gather(vd, r, pstart), gather(vd, r, start)], axis=0)

        def load_state(i, first=first, pieces=pieces, starts=starts, state_rows=state_rows):
            if first:
                return None
            r, _, start, _ = starts(i)
            return tuple(jnp.concatenate([ref[state_rows(r, c, start), :] for c in range(pieces)], axis=0)
                         for ref in (m_s, l_s, acc))

        def store(i, *new, pieces=pieces, plen=plen, starts=starts, state_rows=state_rows):
            r, _, start, _ = starts(i)
            for ref, val in zip((m_s, l_s, acc), new):
                for c in range(pieces):
                    ref[state_rows(r, c, start), :] = val[c * plen:(c + 1) * plen, :]

        run_branch(load_qk, load_v, load_state, store, order(col) - order(row))

    def nat_starts(n):
        return pl.multiple_of(n * w, w), pl.multiple_of(jnp.where(n > 0, n - 1, n) * w, w)

    def nat_qk(n):
        start, pstart = nat_starts(n)
        return (qn[pl.ds(start, w), :],
                jnp.concatenate([kn[pl.ds(pstart, w), :], kn[pl.ds(start, w), :]], axis=0), n > 0)

    def nat_v(n):
        start, pstart = nat_starts(n)
        return jnp.concatenate([v_ref[pl.ds(pstart, w), :], v_ref[pl.ds(start, w), :]], axis=0)

    def nat_state(n):
        return tuple(jnp.concatenate([ref[pl.ds(n * (w // res) + j, res, stride=pitch), :]
                                      for j in range(w // res)], axis=0) for ref in (m_s, l_s, acc))

    def emit(n, m_new, l_new, acc_new):
        o_ref[pl.ds(nat_starts(n)[0], w), :] = (acc_new / l_new).astype(BF16)

    run_branch(nat_qk, nat_v, nat_state, emit, col - row)


def _attention(za, batch):
    n, w3 = za.shape
    seq = n // batch
    width = w3 // 3
    heads = width // HEAD_DIM
    assert seq % (ATT_STEPS * ATT_RESIDUES) == 0
    per = seq // ATT_RESIDUES
    pitch = per + 8
    za3 = za.reshape(batch, seq, w3)
    blk = (None, seq, HEAD_DIM)
    out = pl.pallas_call(
        _att_kernel,
        grid=(batch, heads),
        in_specs=[
            pl.BlockSpec(blk, lambda b, h: (b, 0, h)),
            pl.BlockSpec(blk, lambda b, h: (b, 0, heads + h)),
            pl.BlockSpec(blk, lambda b, h: (b, 0, 2 * heads + h)),
        ],
        out_specs=pl.BlockSpec(blk, lambda b, h: (b, 0, h)),
        out_shape=jax.ShapeDtypeStruct((batch, seq, width), BF16),
        scratch_shapes=[pltpu.VMEM((ATT_RESIDUES, per, HEAD_DIM), BF16) for _ in range(3)]
        + [pltpu.VMEM((ATT_RESIDUES * pitch, HEAD_DIM), F32) for _ in range(3)]
        + [pltpu.VMEM((ATT_GROUP, ATT_STEPS, 2 * ATT_STEPS), F32), pltpu.VMEM((ATT_GROUP, ATT_STEPS, HEAD_DIM), F32)],
        compiler_params=_params("parallel", "parallel"),
        name="dilated_attention",
    )(za3, za3, za3)
    return out.reshape(n, width)


def _gla_kernel(q_ref, k_ref, v_ref, g_ref, r_ref, wg_ref, bg_ref, og_ref, o_ref, st_ref, o_acc):
    rows = q_ref.shape[0]
    ck = GLA_CHUNK
    dv = HEAD_DIM

    @pl.when(pl.program_id(2) == 0)
    def _():
        st_ref[...] = jnp.zeros_like(st_ref)

    x = _dot(r_ref[...], wg_ref[...]) + bg_ref[...]
    log_a = (jnp.minimum(x, 0.0) - jnp.log(1.0 + jnp.exp(-jnp.abs(x)))) / GLA_GATE_NORMALIZER
    hi = log_a.astype(BF16)
    lo = (log_a - hi.astype(F32)).astype(BF16)
    grp = 4 * ck
    r_i = lax.broadcasted_iota(jnp.int32, (grp, grp), 0)
    c_i = lax.broadcasted_iota(jnp.int32, (grp, grp), 1)
    tri = jnp.where(((r_i // ck) == (c_i // ck)) & (c_i <= r_i), 1.0, 0.0).astype(BF16)
    b = jnp.concatenate([_dot(tri, hi[s * grp:(s + 1) * grp]) + _dot(tri, lo[s * grp:(s + 1) * grp])
                         for s in range(rows // grp)], axis=0)

    q = q_ref[...].astype(F32) * GLA_DK ** -0.5
    k = k_ref[...].astype(F32)
    q_dec = (q * jnp.exp(b)).astype(BF16)
    k_inv = (k * jnp.exp(-b)).astype(BF16)

    lane = lax.broadcasted_iota(jnp.int32, (1, 2 * GLA_DK), 1)
    head0 = lane < GLA_DK
    ar = lax.broadcasted_iota(jnp.int32, (ck, 2 * ck), 0)
    ac = lax.broadcasted_iota(jnp.int32, (ck, 2 * ck), 1)
    causal = (ac % ck) <= ar
    sr = lax.broadcasted_iota(jnp.int32, (2 * dv, 2 * GLA_DK), 0)
    sc = lax.broadcasted_iota(jnp.int32, (2 * dv, 2 * GLA_DK), 1)
    own = (sr < dv) == (sc < GLA_DK)
    zeros_v = jnp.zeros((ck, dv), BF16)

    state = st_ref[...]
    for c in range(rows // ck):
        rs = slice(c * ck, (c + 1) * ck)
        qd = q_dec[rs]
        ki = k_inv[rs]
        ki2 = jnp.concatenate([jnp.where(head0, ki, 0), jnp.where(head0, 0, ki)], axis=0)
        a = jnp.where(causal, _dot_nt(qd, ki2), 0.0)
        vp = v_ref[rs, :]
        v_bd = jnp.concatenate(
            [jnp.concatenate([vp[:, :dv], zeros_v], axis=1),
             jnp.concatenate([zeros_v, vp[:, dv:]], axis=1)], axis=0)
        o_acc[rs, :] = _dot(a.astype(BF16), v_bd) + _dot_nt(qd, state.astype(BF16))
        b_last = b[(c + 1) * ck - 1:(c + 1) * ck, :]
        k_dec = (k[rs] * jnp.exp(b_last - b[rs])).astype(BF16)
        state = state * jnp.exp(b_last) + jnp.where(own, _dot_tn(vp, k_dec), 0.0)
    st_ref[...] = state

    for h in range(2):
        cs = slice(h * dv, (h + 1) * dv)
        g = g_ref[:, cs].astype(F32)
        o = _rms_norm(o_acc[:, cs], og_ref[...]) * (g * jax.nn.sigmoid(g))
        o_ref[:, cs] = o.astype(BF16)


def _gla(zg, w_gate, b_gate, out_gain, l, batch):
    n, _ = zg.shape
    seq = n // batch
    dkp = 2 * GLA_DK
    dvp = 2 * HEAD_DIM
    pairs = w_gate.shape[-1] // dkp
    width = pairs * dvp
    rows = min(GLA_ROWS, seq)
    zg3 = zg.reshape(batch, seq, zg.shape[1])
    qk_blocks = 2 * pairs * dkp // dvp
    out = pl.pallas_call(
        _gla_kernel,
        grid=(batch, pairs, seq // rows),
        in_specs=[
            pl.BlockSpec((None, rows, dkp), lambda b, p, t: (b, t, p)),
            pl.BlockSpec((None, rows, dkp), lambda b, p, t: (b, t, pairs + p)),
            pl.BlockSpec((None, rows, dvp), lambda b, p, t: (b, t, qk_blocks + p)),
            pl.BlockSpec((None, rows, dvp), lambda b, p, t: (b, t, qk_blocks + pairs + p)),
            pl.BlockSpec((None, rows, LANES), lambda b, p, t: (b, t, (2 * pairs * dkp + 2 * width) // LANES)),
            pl.BlockSpec((None, LANES, dkp), lambda b, p, t: (l, 0, p)),
            pl.BlockSpec((None, 1, dkp), lambda b, p, t: (l, 0, p)),
            pl.BlockSpec((None, 1, HEAD_DIM), lambda b, p, t: (l, 0, 0)),
        ],
        out_specs=pl.BlockSpec((None, rows, dvp), lambda b, p, t: (b, t, p)),
        out_shape=jax.ShapeDtypeStruct((batch, seq, width), BF16),
        scratch_shapes=[pltpu.VMEM((dvp, dkp), F32), pltpu.VMEM((rows, dvp), F32)],
        compiler_params=_params("parallel", "parallel", "arbitrary"),
        name="gated_linear_attention",
    )(zg3, zg3, zg3, zg3, zg3, w_gate, b_gate, out_gain)
    return out.reshape(n, width)


def _outproj_kernel(x_ref, ya_ref, yb_ref, yc_ref, w_ref, o_ref):
    acc = x_ref[...]
    c = 0
    for y_ref in (ya_ref, yb_ref, yc_ref):
        wdt = y_ref.shape[1]
        acc = acc + _dot(y_ref[...], w_ref[c:c + wdt, :])
        c += wdt
    o_ref[...] = acc


def _outproj(x, ys, w_out, l):
    n, d = x.shape
    tm = min(PROJ_TM, n)
    return pl.pallas_call(
        _outproj_kernel,
        grid=(n // tm,),
        in_specs=[pl.BlockSpec((tm, d), lambda i: (i, 0))]
        + [pl.BlockSpec((tm, y.shape[1]), lambda i: (i, 0)) for y in ys]
        + [pl.BlockSpec((None, w_out.shape[1], d), lambda i: (l, 0, 0), pipeline_mode=pl.Buffered(1))],
        out_specs=pl.BlockSpec((tm, d), lambda i: (i, 0)),
        out_shape=jax.ShapeDtypeStruct((n, d), F32),
        compiler_params=_params("parallel"),
        name="out_proj",
    )(x, *ys, w_out)


def _rope_tables(seq):
    pos = jnp.arange(seq, dtype=F32)
    inv_freq = ROPE_THETA ** (-jnp.arange(0, ROPE_DIMS, 2, dtype=F32) / ROPE_DIMS)
    ang = pos[:, None] * inv_freq[None, :]
    cos, sin = jnp.cos(ang), jnp.sin(ang)
    rest = HEAD_DIM - ROPE_DIMS
    cos_t = jnp.concatenate([cos, cos, jnp.ones((seq, rest), F32)], axis=1)
    sin_t = jnp.concatenate([-sin, sin, jnp.zeros((seq, rest), F32)], axis=1)
    return cos_t, sin_t


def kernel(x, ffn_norm, ffn_w_gate, ffn_w_up, ffn_w_down, mix_norm, w_in, sg_norm, sg_w, sg_b, q_norm, k_norm, gla_w_gate, gla_b_gate, gla_out_norm, w_out):
    batch, seq, d = x.shape
    depth = w_in.shape[0]
    n = batch * seq
    sg_width = sg_norm.shape[-1]
    gla_qk = gla_w_gate.shape[-1]
    gla_width = (gla_qk // GLA_DK) * HEAD_DIM
    att_width = d - sg_width - gla_width
    n_in = w_in.shape[-1]
    assert n_in == 2 * sg_width + 3 * att_width + 2 * gla_qk + 2 * gla_width + GLA_GATE_RANK
    widths = (sg_width, 3 * att_width, 2 * gla_qk + 2 * gla_width + LANES)

    wg, wu, wd = _ffn_column_blocks(ffn_w_gate), _ffn_column_blocks(ffn_w_up), ffn_w_down.astype(BF16)
    w_in_p = jnp.pad(w_in, ((0, 0), (0, 0), (0, LANES - GLA_GATE_RANK))).astype(BF16)
    w_out_b = w_out.astype(BF16)
    gla_wg = jnp.pad(gla_w_gate, ((0, 0), (0, LANES - GLA_GATE_RANK), (0, 0))).astype(BF16)
    ffn_g = ffn_norm.reshape(depth, 2, 1, d)
    mix_g = mix_norm.reshape(depth, 1, d)
    sg_g = sg_norm.reshape(depth, 1, sg_width)
    sg_bt = jnp.swapaxes(sg_b, 1, 2)
    q_g = q_norm.reshape(depth, 1, HEAD_DIM)
    k_g = k_norm.reshape(depth, 1, HEAD_DIM)
    gla_bg = gla_b_gate.reshape(depth, 1, gla_qk)
    gla_og = gla_out_norm.reshape(depth, 1, HEAD_DIM)
    cos_t, sin_t = _rope_tables(seq)

    h = x.reshape(n, d)
    for l in range(depth):
        h = _ffn(h, ffn_g, wg, wu, wd, l, 0)
        ya, za, zg = _inproj(h, mix_g, w_in_p, cos_t, sin_t, q_g, k_g, sg_g, sg_w, sg_bt, l, widths)
        yb = _attention(za, batch)
        yc = _gla(zg, gla_wg, gla_bg, gla_og, l, batch)
        h = _outproj(h, (ya, yb, yc), w_out_b, l)
        h = _ffn(h, ffn_g, wg, wu, wd, l, 1)
    return h.reshape(batch, seq, d)
```

```python
import functools

import jax
import jax.numpy as jnp
from jax import lax
from jax.experimental import pallas as pl
from jax.experimental.pallas import tpu as pltpu

F32 = jnp.float32
BF16 = jnp.bfloat16

EPS = 1e-6
HEAD_DIM = 128
LANES = 128
SG_CHUNK = 128
ATT_DILATIONS = (1, 4, 16)
ATT_STEPS = 128
ATT_RESIDUES = 16
ROPE_THETA = 500000.0
ROPE_DIMS = HEAD_DIM // 4
GLA_DK = 64
GLA_CHUNK = 64
GLA_GATE_RANK = 16
GLA_GATE_NORMALIZER = 16.0
MASK_VALUE = -1e30

VMEM_LIMIT_BYTES = 56 * 1024 * 1024

FFN_TM, FFN_TF = 1024, 512
FFN_NORM_CHUNKS = 4
PROJ_TM = 512
GLA_ROWS = 1024
ATT_PREP_ROWS = 256
ATT_GROUP = 8
LOG2_E = 1.4426950408889634


def _params(*semantics):
    return pltpu.CompilerParams(dimension_semantics=semantics, vmem_limit_bytes=VMEM_LIMIT_BYTES)


def _rms_norm(x, g):
    return x * lax.rsqrt(jnp.mean(x * x, axis=-1, keepdims=True) + EPS) * g


def _dot(a, b):
    return jnp.dot(a, b, preferred_element_type=F32)


def _dot_nt(a, b):
    return lax.dot_general(a, b, (((1,), (1,)), ((), ())), preferred_element_type=F32)


def _dot_tn(a, b):
    return lax.dot_general(a, b, (((0,), (0,)), ((), ())), preferred_element_type=F32)


def _ffn_kernel(x_ref, g_ref, wg_ref, wu_ref, wd_ref, o_ref, h_ref):
    first = pl.program_id(1) == 0

    def half_swiglu(h):
        gate = _dot(h, wg_ref[...])
        up = _dot(h, wu_ref[...])
        return ((gate * jax.nn.sigmoid(gate)) * (0.5 * up)).astype(BF16)

    @pl.when(first)
    def _():
        rows = x_ref.shape[0] // FFN_NORM_CHUNKS
        acts = []
        for c in range(FFN_NORM_CHUNKS):
            rs = slice(c * rows, (c + 1) * rows)
            h = _rms_norm(x_ref[rs, :], g_ref[...]).astype(BF16)
            h_ref[rs, :] = h
            acts.append(half_swiglu(h))
        for c, act in enumerate(acts):
            rs = slice(c * rows, (c + 1) * rows)
            o_ref[rs, :] = x_ref[rs, :] + _dot(act, wd_ref[...])

    @pl.when(jnp.logical_not(first))
    def _():
        o_ref[...] += _dot(half_swiglu(h_ref[...]), wd_ref[...])


def _ffn(x, norm_g, w_gate, w_up, w_down, l, j):
    n, d = x.shape
    f = w_gate.shape[-1]
    tm, tf = min(FFN_TM, n), FFN_TF
    return pl.pallas_call(
        _ffn_kernel,
        grid=(n // tm, f // tf),
        in_specs=[
            pl.BlockSpec((tm, d), lambda i, k: (i, 0)),
            pl.BlockSpec((None, None, 1, d), lambda i, k: (l, j, 0, 0)),
            pl.BlockSpec((None, None, d, tf), lambda i, k: (l, j, 0, k)),
            pl.BlockSpec((None, None, d, tf), lambda i, k: (l, j, 0, k)),
            pl.BlockSpec((None, None, tf, d), lambda i, k: (l, j, k, 0)),
        ],
        out_specs=pl.BlockSpec((tm, d), lambda i, k: (i, 0)),
        out_shape=jax.ShapeDtypeStruct((n, d), F32),
        scratch_shapes=[pltpu.VMEM((tm, d), BF16)],
        compiler_params=_params("parallel", "arbitrary"),
        name="ffn",
    )(x, norm_g, w_gate, w_up, w_down)


def _inproj_kernel(x_ref, g_ref, w_ref, cos_ref, sin_ref, qg_ref, kg_ref, sgn_ref, sgw_ref, sgbt_ref,
                   ya_ref, za_ref, zg_ref):
    h = _rms_norm(x_ref[...], g_ref[...]).astype(BF16)
    rows, sgw = ya_ref.shape
    c0 = 2 * sgw
    c1 = c0 + za_ref.shape[1]
    zs = _dot(h, w_ref[:, :c0])
    za = _dot(h, w_ref[:, c0:c1])
    c2 = c1 + zg_ref.shape[1] - LANES
    zg_ref[:, :c2 - c1] = _dot(h, w_ref[:, c1:c2]).astype(BF16)
    zg_ref[:, c2 - c1:] = jnp.zeros((rows, LANES), BF16)
    zg_ref[:, c2 - c1:c2 - c1 + GLA_GATE_RANK] = _dot(h, w_ref[:, c2:]).astype(BF16)

    heads = za_ref.shape[1] // (3 * HEAD_DIM)
    lane = lax.broadcasted_iota(jnp.int32, (1, HEAD_DIM), 1)
    half = ROPE_DIMS // 2
    cos = cos_ref[...]
    sin = sin_ref[...]
    for j in range(2 * heads):
        cols = slice(j * HEAD_DIM, (j + 1) * HEAD_DIM)
        gain, scale = (qg_ref, HEAD_DIM ** -0.5 * LOG2_E) if j < heads else (kg_ref, None)
        t = _rms_norm(za[:, cols], gain[...])
        partner = jnp.where(lane < half, pltpu.roll(t, HEAD_DIM - half, 1), pltpu.roll(t, half, 1))
        t = t * cos + partner * sin
        if scale is not None:
            t = t * scale
        za_ref[:, cols] = t.astype(BF16)
    za_ref[:, 2 * heads * HEAD_DIM:] = za[:, 2 * heads * HEAD_DIM:].astype(BF16)

    t = SG_CHUNK
    chunks = rows // t
    causal = lax.broadcasted_iota(jnp.int32, (t, t), 1) <= lax.broadcasted_iota(jnp.int32, (t, t), 0)
    for g in range(sgw // HEAD_DIM):
        cu = slice(g * HEAD_DIM, (g + 1) * HEAD_DIM)
        cv = slice(sgw + g * HEAD_DIM, sgw + (g + 1) * HEAD_DIM)
        v = _rms_norm(jax.nn.gelu(zs[:, cv]), sgn_ref[:, cu]).astype(BF16)
        v = jnp.concatenate([v[c * t:(c + 1) * t] for c in range(chunks)], axis=1)
        sv = _dot(jnp.where(causal, sgw_ref[g], 0.0).astype(BF16), v) + sgbt_ref[:, g:g + 1]
        sv = jnp.concatenate([sv[:, c * HEAD_DIM:(c + 1) * HEAD_DIM] for c in range(chunks)], axis=0)
        ya_ref[:, cu] = (jax.nn.gelu(zs[:, cu]) * sv).astype(BF16)


def _inproj(x, norm_g, w_in, cos_t, sin_t, q_gain, k_gain, sg_norm, sg_w, sg_bt, l, widths):
    n, d = x.shape
    c = w_in.shape[-1]
    seq = cos_t.shape[0]
    sgw = sg_norm.shape[-1]
    groups = sgw // HEAD_DIM
    tm = min(PROJ_TM, seq)
    assert seq % tm == 0 and tm % SG_CHUNK == 0

    def layer(*block):
        return pl.BlockSpec((None,) + block, lambda i: (l,) + (0,) * len(block))

    return pl.pallas_call(
        _inproj_kernel,
        grid=(n // tm,),
        in_specs=[
            pl.BlockSpec((tm, d), lambda i: (i, 0)),
            layer(1, d),
            pl.BlockSpec((None, d, c), lambda i: (l, 0, 0), pipeline_mode=pl.Buffered(1)),
            pl.BlockSpec((tm, HEAD_DIM), lambda i: (i % (seq // tm), 0)),
            pl.BlockSpec((tm, HEAD_DIM), lambda i: (i % (seq // tm), 0)),
            layer(1, HEAD_DIM),
            layer(1, HEAD_DIM),
            layer(1, sgw),
            layer(groups, SG_CHUNK, SG_CHUNK),
            layer(SG_CHUNK, groups),
        ],
        out_specs=[pl.BlockSpec((tm, w), lambda i: (i, 0)) for w in widths],
        out_shape=[jax.ShapeDtypeStruct((n, w), BF16) for w in widths],
        compiler_params=_params("parallel"),
        name="in_proj",
    )(x, norm_g, w_in, cos_t, sin_t, q_gain, k_gain, sg_norm, sg_w, sg_bt)


def _att_kernel(qn, kn, v_ref, o_ref, qd, kd, vd, acc, m_s, l_s, s_buf, mb_buf):
    seq = qn.shape[0]
    w = ATT_STEPS
    res = ATT_RESIDUES
    per = seq // res
    pitch = acc.shape[0] // res
    pr = min(ATT_PREP_ROWS, seq)

    def stage(src, dst):
        def body(c, carry):
            t = src[pl.ds(pl.multiple_of(c * pr, pr), pr), :].astype(F32)
            for j in range(pr // res):
                acc[pl.ds(c * (pr // res) + j, res, stride=pitch), :] = t[j * res:(j + 1) * res, :]
            return carry

        lax.fori_loop(0, seq // pr, body, 0, unroll=2)

        def repack(r, carry):
            dst[r] = acc[pl.ds(pl.multiple_of(r * pitch, 8), per), :].astype(BF16)
            return carry

        lax.fori_loop(0, res, repack, 0)

    stage(qn, qd)
    stage(kn, kd)
    stage(v_ref, vd)

    row = lax.broadcasted_iota(jnp.int32, (w, w), 0)
    col = lax.broadcasted_iota(jnp.int32, (w, w), 1)
    group = s_buf.shape[0]

    def run_branch(load_qk, load_v, load_state, store, order_diff):
        def scores(g):
            for u in range(group):
                q, k2, has_prev = load_qk(g * group + u)
                s = _dot_nt(q, k2)
                prev_min = jnp.where(has_prev, 0, 2 * w)
                s = jnp.concatenate([jnp.where(order_diff >= prev_min, s[:, :w], MASK_VALUE),
                                     jnp.where(order_diff <= 0, s[:, w:], MASK_VALUE)], axis=1)
                s_buf[u] = s
                mb_buf[u] = jnp.broadcast_to(jnp.max(s, axis=1, keepdims=True), (w, HEAD_DIM))

        def consume(g):
            for u in range(group):
                i = g * group + u
                old = load_state(i)
                m_blk = mb_buf[u]
                m_new = m_blk if old is None else jnp.maximum(old[0], m_blk)
                p = jnp.exp2(s_buf[u] - jnp.concatenate([m_new, m_new], axis=1))
                p_sum = jnp.sum(p, axis=1, keepdims=True)
                pv = _dot(p.astype(BF16), load_v(i))
                if old is None:
                    store(i, m_new, jnp.broadcast_to(p_sum, (w, HEAD_DIM)), pv)
                else:
                    alpha = jnp.exp2(old[0] - m_new)
                    store(i, m_new, alpha * old[1] + p_sum, alpha * old[2] + pv)

        def body(g, carry):
            consume(g - 1)
            scores(g)
            return carry

        groups = seq // (w * group)
        scores(0)
        lax.fori_loop(1, groups, body, 0)
        consume(groups - 1)

    for d in ATT_DILATIONS[:0:-1]:
        first = d == ATT_DILATIONS[-1]
        pieces = res // d
        plen = w // pieces
        nb = per // plen
        order = lambda x: pieces * (x % plen) + x // plen

        def gather(ref, r, at, d=d, pieces=pieces, plen=plen):
            return jnp.concatenate([ref[c * d + r, pl.ds(at, plen), :] for c in range(pieces)], axis=0)

        def starts(i, nb=nb, plen=plen):
            n = i % nb
            return (i // nb, n, pl.multiple_of(n * plen, plen),
                    pl.multiple_of(jnp.where(n > 0, n - 1, n) * plen, plen))

        def state_rows(r, c, start, d=d, plen=plen):
            return pl.ds(pl.multiple_of((c * d + r) * pitch + start, 8), plen)

        def load_qk(i, gather=gather, starts=starts):
            r, n, start, pstart = starts(i)
            return gather(qd, r, start), jnp.concatenate([gather(kd, r, pstart), gather(kd, r, start)], axis=0), n > 0

        def load_v(i, gather=gather, starts=starts):
            r, _, start, pstart = starts(i)
            return jnp.concatenate([gather(vd, r, pstart), gather(vd, r, start)], axis=0)

        def load_state(i, first=first, pieces=pieces, starts=starts, state_rows=state_rows):
            if first:
                return None
            r, _, start, _ = starts(i)
            return tuple(jnp.concatenate([ref[state_rows(r, c, start), :] for c in range(pieces)], axis=0)
                         for ref in (m_s, l_s, acc))

        def store(i, *new, pieces=pieces, plen=plen, starts=starts, state_rows=state_rows):
            r, _, start, _ = starts(i)
            for ref, val in zip((m_s, l_s, acc), new):
                for c in range(pieces):
                    ref[state_rows(r, c, start), :] = val[c * plen:(c + 1) * plen, :]

        run_branch(load_qk, load_v, load_state, store, order(col) - order(row))

    def nat_starts(n):
        return pl.multiple_of(n * w, w), pl.multiple_of(jnp.where(n > 0, n - 1, n) * w, w)

    def nat_qk(n):
        start, pstart = nat_starts(n)
        return (qn[pl.ds(start, w), :],
                jnp.concatenate([kn[pl.ds(pstart, w), :], kn[pl.ds(start, w), :]], axis=0), n > 0)

    def nat_v(n):
        start, pstart = nat_starts(n)
        return jnp.concatenate([v_ref[pl.ds(pstart, w), :], v_ref[pl.ds(start, w), :]], axis=0)

    def nat_state(n):
        return tuple(jnp.concatenate([ref[pl.ds(n * (w // res) + j, res, stride=pitch), :]
                                      for j in range(w // res)], axis=0) for ref in (m_s, l_s, acc))

    def emit(n, m_new, l_new, acc_new):
        o_ref[pl.ds(nat_starts(n)[0], w), :] = (acc_new / l_new).astype(BF16)

    run_branch(nat_qk, nat_v, nat_state, emit, col - row)


def _attention(za, batch):
    n, w3 = za.shape
    seq = n // batch
    width = w3 // 3
    heads = width // HEAD_DIM
    assert seq % (ATT_STEPS * ATT_RESIDUES) == 0
    per = seq // ATT_RESIDUES
    pitch = per + 8
    za3 = za.reshape(batch, seq, w3)
    blk = (None, seq, HEAD_DIM)
    out = pl.pallas_call(
        _att_kernel,
        grid=(batch, heads),
        in_specs=[
            pl.BlockSpec(blk, lambda b, h: (b, 0, h)),
            pl.BlockSpec(blk, lambda b, h: (b, 0, heads + h)),
            pl.BlockSpec(blk, lambda b, h: (b, 0, 2 * heads + h)),
        ],
        out_specs=pl.BlockSpec(blk, lambda b, h: (b, 0, h)),
        out_shape=jax.ShapeDtypeStruct((batch, seq, width), BF16),
        scratch_shapes=[pltpu.VMEM((ATT_RESIDUES, per, HEAD_DIM), BF16) for _ in range(3)]
        + [pltpu.VMEM((ATT_RESIDUES * pitch, HEAD_DIM), F32) for _ in range(3)]
        + [pltpu.VMEM((ATT_GROUP, ATT_STEPS, 2 * ATT_STEPS), F32), pltpu.VMEM((ATT_GROUP, ATT_STEPS, HEAD_DIM), F32)],
        compiler_params=_params("parallel", "parallel"),
        name="dilated_attention",
    )(za3, za3, za3)
    return out.reshape(n, width)


def _gla_kernel(q_ref, k_ref, v_ref, g_ref, r_ref, wg_ref, bg_ref, og_ref, o_ref, st_ref, o_acc):
    rows = q_ref.shape[0]
    ck = GLA_CHUNK
    dv = HEAD_DIM

    @pl.when(pl.program_id(2) == 0)
    def _():
        st_ref[...] = jnp.zeros_like(st_ref)

    x = _dot(r_ref[...], wg_ref[...]) + bg_ref[...]
    log_a = (jnp.minimum(x, 0.0) - jnp.log(1.0 + jnp.exp(-jnp.abs(x)))) / GLA_GATE_NORMALIZER
    hi = log_a.astype(BF16)
    lo = (log_a - hi.astype(F32)).astype(BF16)
    grp = 4 * ck
    r_i = lax.broadcasted_iota(jnp.int32, (grp, grp), 0)
    c_i = lax.broadcasted_iota(jnp.int32, (grp, grp), 1)
    tri = jnp.where(((r_i // ck) == (c_i // ck)) & (c_i <= r_i), 1.0, 0.0).astype(BF16)
    b = jnp.concatenate([_dot(tri, hi[s * grp:(s + 1) * grp]) + _dot(tri, lo[s * grp:(s + 1) * grp])
                         for s in range(rows // grp)], axis=0)

    q = q_ref[...].astype(F32) * GLA_DK ** -0.5
    k = k_ref[...].astype(F32)
    q_dec = (q * jnp.exp(b)).astype(BF16)
    k_inv = (k * jnp.exp(-b)).astype(BF16)

    lane = lax.broadcasted_iota(jnp.int32, (1, 2 * GLA_DK), 1)
    head0 = lane < GLA_DK
    ar = lax.broadcasted_iota(jnp.int32, (ck, 2 * ck), 0)
    ac = lax.broadcasted_iota(jnp.int32, (ck, 2 * ck), 1)
    causal = (ac % ck) <= ar
    sr = lax.broadcasted_iota(jnp.int32, (2 * dv, 2 * GLA_DK), 0)
    sc = lax.broadcasted_iota(jnp.int32, (2 * dv, 2 * GLA_DK), 1)
    own = (sr < dv) == (sc < GLA_DK)
    zeros_v = jnp.zeros((ck, dv), BF16)

    state = st_ref[...]
    for c in range(rows // ck):
        rs = slice(c * ck, (c + 1) * ck)
        qd = q_dec[rs]
        ki = k_inv[rs]
        ki2 = jnp.concatenate([jnp.where(head0, ki, 0), jnp.where(head0, 0, ki)], axis=0)
        a = jnp.where(causal, _dot_nt(qd, ki2), 0.0)
        vp = v_ref[rs, :]
        v_bd = jnp.concatenate(
            [jnp.concatenate([vp[:, :dv], zeros_v], axis=1),
             jnp.concatenate([zeros_v, vp[:, dv:]], axis=1)], axis=0)
        o_acc[rs, :] = _dot(a.astype(BF16), v_bd) + _dot_nt(qd, state.astype(BF16))
        b_last = b[(c + 1) * ck - 1:(c + 1) * ck, :]
        k_dec = (k[rs] * jnp.exp(b_last - b[rs])).astype(BF16)
        state = state * jnp.exp(b_last) + jnp.where(own, _dot_tn(vp, k_dec), 0.0)
    st_ref[...] = state

    for h in range(2):
        cs = slice(h * dv, (h + 1) * dv)
        g = g_ref[:, cs].astype(F32)
        o = _rms_norm(o_acc[:, cs], og_ref[...]) * (g * jax.nn.sigmoid(g))
        o_ref[:, cs] = o.astype(BF16)


def _gla(zg, w_gate, b_gate, out_gain, l, batch):
    n, _ = zg.shape
    seq = n // batch
    dkp = 2 * GLA_DK
    dvp = 2 * HEAD_DIM
    pairs = w_gate.shape[-1] // dkp
    width = pairs * dvp
    rows = min(GLA_ROWS, seq)
    zg3 = zg.reshape(batch, seq, zg.shape[1])
    qk_blocks = 2 * pairs * dkp // dvp
    out = pl.pallas_call(
        _gla_kernel,
        grid=(batch, pairs, seq // rows),
        in_specs=[
            pl.BlockSpec((None, rows, dkp), lambda b, p, t: (b, t, p)),
            pl.BlockSpec((None, rows, dkp), lambda b, p, t: (b, t, pairs + p)),
            pl.BlockSpec((None, rows, dvp), lambda b, p, t: (b, t, qk_blocks + p)),
            pl.BlockSpec((None, rows, dvp), lambda b, p, t: (b, t, qk_blocks + pairs + p)),
            pl.BlockSpec((None, rows, LANES), lambda b, p, t: (b, t, (2 * pairs * dkp + 2 * width) // LANES)),
            pl.BlockSpec((None, LANES, dkp), lambda b, p, t: (l, 0, p)),
            pl.BlockSpec((None, 1, dkp), lambda b, p, t: (l, 0, p)),
            pl.BlockSpec((None, 1, HEAD_DIM), lambda b, p, t: (l, 0, 0)),
        ],
        out_specs=pl.BlockSpec((None, rows, dvp), lambda b, p, t: (b, t, p)),
        out_shape=jax.ShapeDtypeStruct((batch, seq, width), BF16),
        scratch_shapes=[pltpu.VMEM((dvp, dkp), F32), pltpu.VMEM((rows, dvp), F32)],
        compiler_params=_params("parallel", "parallel", "arbitrary"),
        name="gated_linear_attention",
    )(zg3, zg3, zg3, zg3, zg3, w_gate, b_gate, out_gain)
    return out.reshape(n, width)


def _outproj_kernel(x_ref, ya_ref, yb_ref, yc_ref, w_ref, o_ref):
    acc = x_ref[...]
    c = 0
    for y_ref in (ya_ref, yb_ref, yc_ref):
        wdt = y_ref.shape[1]
        acc = acc + _dot(y_ref[...], w_ref[c:c + wdt, :])
        c += wdt
    o_ref[...] = acc


def _outproj(x, ys, w_out, l):
    n, d = x.shape
    tm = min(PROJ_TM, n)
    return pl.pallas_call(
        _outproj_kernel,
        grid=(n // tm,),
        in_specs=[pl.BlockSpec((tm, d), lambda i: (i, 0))]
        + [pl.BlockSpec((tm, y.shape[1]), lambda i: (i, 0)) for y in ys]
        + [pl.BlockSpec((None, w_out.shape[1], d), lambda i: (l, 0, 0), pipeline_mode=pl.Buffered(1))],
        out_specs=pl.BlockSpec((tm, d), lambda i: (i, 0)),
        out_shape=jax.ShapeDtypeStruct((n, d), F32),
        compiler_params=_params("parallel"),
        name="out_proj",
    )(x, *ys, w_out)


def _rope_tables(seq):
    pos = jnp.arange(seq, dtype=F32)
    inv_freq = ROPE_THETA ** (-jnp.arange(0, ROPE_DIMS, 2, dtype=F32) / ROPE_DIMS)
    ang = pos[:, None] * inv_freq[None, :]
    cos, sin = jnp.cos(ang), jnp.sin(ang)
    rest = HEAD_DIM - ROPE_DIMS
    cos_t = jnp.concatenate([cos, cos, jnp.ones((seq, rest), F32)], axis=1)
    sin_t = jnp.concatenate([-sin, sin, jnp.zeros((seq, rest), F32)], axis=1)
    return cos_t, sin_t


def kernel(x, ffn_norm, ffn_w_gate, ffn_w_up, ffn_w_down, mix_norm, w_in, sg_norm, sg_w, sg_b, q_norm, k_norm, gla_w_gate, gla_b_gate, gla_out_norm, w_out):
    batch, seq, d = x.shape
    depth = w_in.shape[0]
    n = batch * seq
    sg_width = sg_norm.shape[-1]
    gla_qk = gla_w_gate.shape[-1]
    gla_width = (gla_qk // GLA_DK) * HEAD_DIM
    att_width = d - sg_width - gla_width
    n_in = w_in.shape[-1]
    assert n_in == 2 * sg_width + 3 * att_width + 2 * gla_qk + 2 * gla_width + GLA_GATE_RANK
    widths = (sg_width, 3 * att_width, 2 * gla_qk + 2 * gla_width + LANES)

    wg, wu, wd = ffn_w_gate.astype(BF16), ffn_w_up.astype(BF16), ffn_w_down.astype(BF16)
    w_in_p = w_in.astype(BF16)
    w_out_b = w_out.astype(BF16)
    gla_wg = jnp.pad(gla_w_gate, ((0, 0), (0, LANES - GLA_GATE_RANK), (0, 0))).astype(BF16)
    ffn_g = ffn_norm.reshape(depth, 2, 1, d)
    mix_g = mix_norm.reshape(depth, 1, d)
    sg_g = sg_norm.reshape(depth, 1, sg_width)
    sg_bt = jnp.swapaxes(sg_b, 1, 2)
    q_g = q_norm.reshape(depth, 1, HEAD_DIM)
    k_g = k_norm.reshape(depth, 1, HEAD_DIM)
    gla_bg = gla_b_gate.reshape(depth, 1, gla_qk)
    gla_og = gla_out_norm.reshape(depth, 1, HEAD_DIM)
    cos_t, sin_t = _rope_tables(seq)

    h = x.reshape(n, d)
    for l in range(depth):
        h = _ffn(h, ffn_g, wg, wu, wd, l, 0)
        ya, za, zg = _inproj(h, mix_g, w_in_p, cos_t, sin_t, q_g, k_g, sg_g, sg_w, sg_bt, l, widths)
        yb = _attention(za, batch)
        yc = _gla(zg, gla_wg, gla_bg, gla_og, l, batch)
        h = _outproj(h, (ya, yb, yc), w_out_b, l)
        h = _ffn(h, ffn_g, wg, wu, wd, l, 1)
    return h.reshape(batch, seq, d)
```

```python
import functools

import jax
import jax.numpy as jnp
from jax import lax
from jax.experimental import pallas as pl
from jax.experimental.pallas import tpu as pltpu

F32 = jnp.float32
BF16 = jnp.bfloat16

EPS = 1e-6
HEAD_DIM = 128
LANES = 128
SG_CHUNK = 128
ATT_DILATIONS = (1, 4, 16)
ATT_STEPS = 128
ATT_RESIDUES = 16
ROPE_THETA = 500000.0
ROPE_DIMS = HEAD_DIM // 4
GLA_DK = 64
GLA_CHUNK = 64
GLA_GATE_RANK = 16
GLA_GATE_NORMALIZER = 16.0
MASK_VALUE = -1e30

VMEM_LIMIT_BYTES = 56 * 1024 * 1024

FFN_TM, FFN_TF = 1024, 512
FFN_NORM_CHUNKS = 4
PROJ_TM = 512
GLA_ROWS = 2048
ATT_PREP_ROWS = 256
ATT_GROUP = 8
LOG2_E = 1.4426950408889634


def _params(*semantics):
    return pltpu.CompilerParams(dimension_semantics=semantics, vmem_limit_bytes=VMEM_LIMIT_BYTES)


def _rms_norm(x, g):
    return x * lax.rsqrt(jnp.mean(x * x, axis=-1, keepdims=True) + EPS) * g


def _dot(a, b):
    return jnp.dot(a, b, preferred_element_type=F32)


def _dot_nt(a, b):
    return lax.dot_general(a, b, (((1,), (1,)), ((), ())), preferred_element_type=F32)


def _dot_tn(a, b):
    return lax.dot_general(a, b, (((0,), (0,)), ((), ())), preferred_element_type=F32)


def _ffn_kernel(x_ref, g_ref, wg_ref, wu_ref, wd_ref, o_ref, h_ref):
    first = pl.program_id(1) == 0

    def half_swiglu(h):
        gate = _dot(h, wg_ref[...])
        up = _dot(h, wu_ref[...])
        return ((gate * jax.nn.sigmoid(gate)) * (0.5 * up)).astype(BF16)

    @pl.when(first)
    def _():
        rows = x_ref.shape[0] // FFN_NORM_CHUNKS
        acts = []
        for c in range(FFN_NORM_CHUNKS):
            rs = slice(c * rows, (c + 1) * rows)
            h = _rms_norm(x_ref[rs, :], g_ref[...]).astype(BF16)
            h_ref[rs, :] = h
            acts.append(half_swiglu(h))
        for c, act in enumerate(acts):
            rs = slice(c * rows, (c + 1) * rows)
            o_ref[rs, :] = x_ref[rs, :] + _dot(act, wd_ref[...])

    @pl.when(jnp.logical_not(first))
    def _():
        o_ref[...] += _dot(half_swiglu(h_ref[...]), wd_ref[...])


def _ffn(x, norm_g, w_gate, w_up, w_down, l, j):
    n, d = x.shape
    f = w_gate.shape[-1]
    tm, tf = min(FFN_TM, n), FFN_TF
    return pl.pallas_call(
        _ffn_kernel,
        grid=(n // tm, f // tf),
        in_specs=[
            pl.BlockSpec((tm, d), lambda i, k: (i, 0)),
            pl.BlockSpec((None, None, 1, d), lambda i, k: (l, j, 0, 0)),
            pl.BlockSpec((None, None, d, tf), lambda i, k: (l, j, 0, k)),
            pl.BlockSpec((None, None, d, tf), lambda i, k: (l, j, 0, k)),
            pl.BlockSpec((None, None, tf, d), lambda i, k: (l, j, k, 0)),
        ],
        out_specs=pl.BlockSpec((tm, d), lambda i, k: (i, 0)),
        out_shape=jax.ShapeDtypeStruct((n, d), F32),
        scratch_shapes=[pltpu.VMEM((tm, d), BF16)],
        compiler_params=_params("parallel", "arbitrary"),
        name="ffn",
    )(x, norm_g, w_gate, w_up, w_down)


def _inproj_kernel(x_ref, g_ref, w_ref, cos_ref, sin_ref, qg_ref, kg_ref, sgn_ref, sgw_ref, sgbt_ref,
                   ya_ref, za_ref, zg_ref):
    h = _rms_norm(x_ref[...], g_ref[...]).astype(BF16)
    rows, sgw = ya_ref.shape
    c0 = 2 * sgw
    c1 = c0 + za_ref.shape[1]
    zs = _dot(h, w_ref[:, :c0])
    za = _dot(h, w_ref[:, c0:c1])
    c2 = c1 + zg_ref.shape[1] - LANES
    zg_ref[:, :c2 - c1] = _dot(h, w_ref[:, c1:c2]).astype(BF16)
    zg_ref[:, c2 - c1:] = jnp.zeros((rows, LANES), BF16)
    zg_ref[:, c2 - c1:c2 - c1 + GLA_GATE_RANK] = _dot(h, w_ref[:, c2:]).astype(BF16)

    heads = za_ref.shape[1] // (3 * HEAD_DIM)
    lane = lax.broadcasted_iota(jnp.int32, (1, HEAD_DIM), 1)
    half = ROPE_DIMS // 2
    cos = cos_ref[...]
    sin = sin_ref[...]
    for j in range(2 * heads):
        cols = slice(j * HEAD_DIM, (j + 1) * HEAD_DIM)
        gain, scale = (qg_ref, HEAD_DIM ** -0.5 * LOG2_E) if j < heads else (kg_ref, None)
        t = _rms_norm(za[:, cols], gain[...])
        partner = jnp.where(lane < half, pltpu.roll(t, HEAD_DIM - half, 1), pltpu.roll(t, half, 1))
        t = t * cos + partner * sin
        if scale is not None:
            t = t * scale
        za_ref[:, cols] = t.astype(BF16)
    za_ref[:, 2 * heads * HEAD_DIM:] = za[:, 2 * heads * HEAD_DIM:].astype(BF16)

    t = SG_CHUNK
    chunks = rows // t
    causal = lax.broadcasted_iota(jnp.int32, (t, t), 1) <= lax.broadcasted_iota(jnp.int32, (t, t), 0)
    for g in range(sgw // HEAD_DIM):
        cu = slice(g * HEAD_DIM, (g + 1) * HEAD_DIM)
        cv = slice(sgw + g * HEAD_DIM, sgw + (g + 1) * HEAD_DIM)
        v = _rms_norm(jax.nn.gelu(zs[:, cv]), sgn_ref[:, cu]).astype(BF16)
        v = jnp.concatenate([v[c * t:(c + 1) * t] for c in range(chunks)], axis=1)
        sv = _dot(jnp.where(causal, sgw_ref[g], 0.0).astype(BF16), v) + sgbt_ref[:, g:g + 1]
        sv = jnp.concatenate([sv[:, c * HEAD_DIM:(c + 1) * HEAD_DIM] for c in range(chunks)], axis=0)
        ya_ref[:, cu] = (jax.nn.gelu(zs[:, cu]) * sv).astype(BF16)


def _inproj(x, norm_g, w_in, cos_t, sin_t, q_gain, k_gain, sg_norm, sg_w, sg_bt, l, widths):
    n, d = x.shape
    c = w_in.shape[-1]
    seq = cos_t.shape[0]
    sgw = sg_norm.shape[-1]
    groups = sgw // HEAD_DIM
    tm = min(PROJ_TM, seq)
    assert seq % tm == 0 and tm % SG_CHUNK == 0

    def layer(*block):
        return pl.BlockSpec((None,) + block, lambda i: (l,) + (0,) * len(block))

    return pl.pallas_call(
        _inproj_kernel,
        grid=(n // tm,),
        in_specs=[
            pl.BlockSpec((tm, d), lambda i: (i, 0)),
            layer(1, d),
            pl.BlockSpec((None, d, c), lambda i: (l, 0, 0), pipeline_mode=pl.Buffered(1)),
            pl.BlockSpec((tm, HEAD_DIM), lambda i: (i % (seq // tm), 0)),
            pl.BlockSpec((tm, HEAD_DIM), lambda i: (i % (seq // tm), 0)),
            layer(1, HEAD_DIM),
            layer(1, HEAD_DIM),
            layer(1, sgw),
            layer(groups, SG_CHUNK, SG_CHUNK),
            layer(SG_CHUNK, groups),
        ],
        out_specs=[pl.BlockSpec((tm, w), lambda i: (i, 0)) for w in widths],
        out_shape=[jax.ShapeDtypeStruct((n, w), BF16) for w in widths],
        compiler_params=_params("parallel"),
        name="in_proj",
    )(x, norm_g, w_in, cos_t, sin_t, q_gain, k_gain, sg_norm, sg_w, sg_bt)


def _att_kernel(qn, kn, v_ref, o_ref, qd, kd, vd, acc, m_s, l_s, s_buf, mb_buf):
    seq = qn.shape[0]
    w = ATT_STEPS
    res = ATT_RESIDUES
    per = seq // res
    pitch = acc.shape[0] // res
    pr = min(ATT_PREP_ROWS, seq)

    def stage(src, dst):
        def body(c, carry):
            t = src[pl.ds(pl.multiple_of(c * pr, pr), pr), :].astype(F32)
            for j in range(pr // res):
                acc[pl.ds(c * (pr // res) + j, res, stride=pitch), :] = t[j * res:(j + 1) * res, :]
            return carry

        lax.fori_loop(0, seq // pr, body, 0, unroll=2)

        def repack(r, carry):
            dst[r] = acc[pl.ds(pl.multiple_of(r * pitch, 8), per), :].astype(BF16)
            return carry

        lax.fori_loop(0, res, repack, 0)

    stage(qn, qd)
    stage(kn, kd)
    stage(v_ref, vd)

    row = lax.broadcasted_iota(jnp.int32, (w, w), 0)
    col = lax.broadcasted_iota(jnp.int32, (w, w), 1)
    group = s_buf.shape[0]

    def run_branch(load_qk, load_v, load_state, store, order_diff):
        def scores(g):
            for u in range(group):
                q, k2, has_prev = load_qk(g * group + u)
                s = _dot_nt(q, k2)
                prev_min = jnp.where(has_prev, 0, 2 * w)
                s = jnp.concatenate([jnp.where(order_diff >= prev_min, s[:, :w], MASK_VALUE),
                                     jnp.where(order_diff <= 0, s[:, w:], MASK_VALUE)], axis=1)
                s_buf[u] = s
                mb_buf[u] = jnp.broadcast_to(jnp.max(s, axis=1, keepdims=True), (w, HEAD_DIM))

        def consume(g):
            for u in range(group):
                i = g * group + u
                old = load_state(i)
                m_blk = mb_buf[u]
                m_new = m_blk if old is None else jnp.maximum(old[0], m_blk)
                p = jnp.exp2(s_buf[u] - jnp.concatenate([m_new, m_new], axis=1))
                p_sum = jnp.sum(p, axis=1, keepdims=True)
                pv = _dot(p.astype(BF16), load_v(i))
                if old is None:
                    store(i, m_new, jnp.broadcast_to(p_sum, (w, HEAD_DIM)), pv)
                else:
                    alpha = jnp.exp2(old[0] - m_new)
                    store(i, m_new, alpha * old[1] + p_sum, alpha * old[2] + pv)

        def body(g, carry):
            consume(g - 1)
            scores(g)
            return carry

        groups = seq // (w * group)
        scores(0)
        lax.fori_loop(1, groups, body, 0)
        consume(groups - 1)

    for d in ATT_DILATIONS[:0:-1]:
        first = d == ATT_DILATIONS[-1]
        pieces = res // d
        plen = w // pieces
        nb = per // plen
        order = lambda x: pieces * (x % plen) + x // plen

        def gather(ref, r, at, d=d, pieces=pieces, plen=plen):
            return jnp.concatenate([ref[c * d + r, pl.ds(at, plen), :] for c in range(pieces)], axis=0)

        def starts(i, nb=nb, plen=plen):
            n = i % nb
            return (i // nb, n, pl.multiple_of(n * plen, plen),
                    pl.multiple_of(jnp.where(n > 0, n - 1, n) * plen, plen))

        def state_rows(r, c, start, d=d, plen=plen):
            return pl.ds(pl.multiple_of((c * d + r) * pitch + start, 8), plen)

        def load_qk(i, gather=gather, starts=starts):
            r, n, start, pstart = starts(i)
            return gather(qd, r, start), jnp.concatenate([gather(kd, r, pstart), gather(kd, r, start)], axis=0), n > 0

        def load_v(i, gather=gather, starts=starts):
            r, _, start, pstart = starts(i)
            return jnp.concatenate([gather(vd, r, pstart), gather(vd, r, start)], axis=0)

        def load_state(i, first=first, pieces=pieces, starts=starts, state_rows=state_rows):
            if first:
                return None
            r, _, start, _ = starts(i)
            return tuple(jnp.concatenate([ref[state_rows(r, c, start), :] for c in range(pieces)], axis=0)
                         for ref in (m_s, l_s, acc))

        def store(i, *new, pieces=pieces, plen=plen, starts=starts, state_rows=state_rows):
            r, _, start, _ = starts(i)
            for ref, val in zip((m_s, l_s, acc), new):
                for c in range(pieces):
                    ref[state_rows(r, c, start), :] = val[c * plen:(c + 1) * plen, :]

        run_branch(load_qk, load_v, load_state, store, order(col) - order(row))

    def nat_starts(n):
        return pl.multiple_of(n * w, w), pl.multiple_of(jnp.where(n > 0, n - 1, n) * w, w)

    def nat_qk(n):
        start, pstart = nat_starts(n)
        return (qn[pl.ds(start, w), :],
                jnp.concatenate([kn[pl.ds(pstart, w), :], kn[pl.ds(start, w), :]], axis=0), n > 0)

    def nat_v(n):
        start, pstart = nat_starts(n)
        return jnp.concatenate([v_ref[pl.ds(pstart, w), :], v_ref[pl.ds(start, w), :]], axis=0)

    def nat_state(n):
        return tuple(jnp.concatenate([ref[pl.ds(n * (w // res) + j, res, stride=pitch), :]
                                      for j in range(w // res)], axis=0) for ref in (m_s, l_s, acc))

    def emit(n, m_new, l_new, acc_new):
        o_ref[pl.ds(nat_starts(n)[0], w), :] = (acc_new / l_new).astype(BF16)

    run_branch(nat_qk, nat_v, nat_state, emit, col - row)


def _attention(za, batch):
    n, w3 = za.shape
    seq = n // batch
    width = w3 // 3
    heads = width // HEAD_DIM
    assert seq % (ATT_STEPS * ATT_RESIDUES) == 0
    per = seq // ATT_RESIDUES
    pitch = per + 8
    za3 = za.reshape(batch, seq, w3)
    blk = (None, seq, HEAD_DIM)
    out = pl.pallas_call(
        _att_kernel,
        grid=(batch, heads),
        in_specs=[
            pl.BlockSpec(blk, lambda b, h: (b, 0, h)),
            pl.BlockSpec(blk, lambda b, h: (b, 0, heads + h)),
            pl.BlockSpec(blk, lambda b, h: (b, 0, 2 * heads + h)),
        ],
        out_specs=pl.BlockSpec(blk, lambda b, h: (b, 0, h)),
        out_shape=jax.ShapeDtypeStruct((batch, seq, width), BF16),
        scratch_shapes=[pltpu.VMEM((ATT_RESIDUES, per, HEAD_DIM), BF16) for _ in range(3)]
        + [pltpu.VMEM((ATT_RESIDUES * pitch, HEAD_DIM), F32) for _ in range(3)]
        + [pltpu.VMEM((ATT_GROUP, ATT_STEPS, 2 * ATT_STEPS), F32), pltpu.VMEM((ATT_GROUP, ATT_STEPS, HEAD_DIM), F32)],
        compiler_params=_params("parallel", "parallel"),
        name="dilated_attention",
    )(za3, za3, za3)
    return out.reshape(n, width)


def _gla_kernel(q_ref, k_ref, v_ref, g_ref, r_ref, wg_ref, bg_ref, og_ref, o_ref, st_ref, o_acc):
    rows = q_ref.shape[0]
    ck = GLA_CHUNK
    dv = HEAD_DIM

    @pl.when(pl.program_id(2) == 0)
    def _():
        st_ref[...] = jnp.zeros_like(st_ref)

    x = _dot(r_ref[...], wg_ref[...]) + bg_ref[...]
    log_a = (jnp.minimum(x, 0.0) - jnp.log(1.0 + jnp.exp(-jnp.abs(x)))) / GLA_GATE_NORMALIZER
    hi = log_a.astype(BF16)
    lo = (log_a - hi.astype(F32)).astype(BF16)
    grp = 4 * ck
    r_i = lax.broadcasted_iota(jnp.int32, (grp, grp), 0)
    c_i = lax.broadcasted_iota(jnp.int32, (grp, grp), 1)
    tri = jnp.where(((r_i // ck) == (c_i // ck)) & (c_i <= r_i), 1.0, 0.0).astype(BF16)
    b = jnp.concatenate([_dot(tri, hi[s * grp:(s + 1) * grp]) + _dot(tri, lo[s * grp:(s + 1) * grp])
                         for s in range(rows // grp)], axis=0)

    q = q_ref[...].astype(F32) * GLA_DK ** -0.5
    k = k_ref[...].astype(F32)
    q_dec = (q * jnp.exp(b)).astype(BF16)
    k_inv = (k * jnp.exp(-b)).astype(BF16)

    lane = lax.broadcasted_iota(jnp.int32, (1, 2 * GLA_DK), 1)
    head0 = lane < GLA_DK
    ar = lax.broadcasted_iota(jnp.int32, (ck, 2 * ck), 0)
    ac = lax.broadcasted_iota(jnp.int32, (ck, 2 * ck), 1)
    causal = (ac % ck) <= ar
    sr = lax.broadcasted_iota(jnp.int32, (2 * dv, 2 * GLA_DK), 0)
    sc = lax.broadcasted_iota(jnp.int32, (2 * dv, 2 * GLA_DK), 1)
    own = (sr < dv) == (sc < GLA_DK)
    zeros_v = jnp.zeros((ck, dv), BF16)

    chunks = [slice(c * ck, (c + 1) * ck) for c in range(rows // ck)]
    scores, incs, decays = [], [], []
    for rs in chunks:
        ki = k_inv[rs]
        ki2 = jnp.concatenate([jnp.where(head0, ki, 0), jnp.where(head0, 0, ki)], axis=0)
        scores.append(_dot_nt(q_dec[rs], ki2))
    for rs in chunks:
        b_last = b[rs.stop - 1:rs.stop, :]
        k_dec = (k[rs] * jnp.exp(b_last - b[rs])).astype(BF16)
        incs.append(_dot_tn(v_ref[rs, :], k_dec))
        decays.append(jnp.exp(b_last))
    for rs, a in zip(chunks, scores):
        vp = v_ref[rs, :]
        v_bd = jnp.concatenate(
            [jnp.concatenate([vp[:, :dv], zeros_v], axis=1),
             jnp.concatenate([zeros_v, vp[:, dv:]], axis=1)], axis=0)
        o_acc[rs, :] = _dot(jnp.where(causal, a, 0.0).astype(BF16), v_bd)
    state = st_ref[...]
    for rs, inc, decay in zip(chunks, incs, decays):
        o_acc[rs, :] += _dot_nt(q_dec[rs], state.astype(BF16))
        state = state * decay + jnp.where(own, inc, 0.0)
    st_ref[...] = state

    for h in range(2):
        cs = slice(h * dv, (h + 1) * dv)
        g = g_ref[:, cs].astype(F32)
        o = _rms_norm(o_acc[:, cs], og_ref[...]) * (g * jax.nn.sigmoid(g))
        o_ref[:, cs] = o.astype(BF16)


def _gla(zg, w_gate, b_gate, out_gain, l, batch):
    n, _ = zg.shape
    seq = n // batch
    dkp = 2 * GLA_DK
    dvp = 2 * HEAD_DIM
    pairs = w_gate.shape[-1] // dkp
    width = pairs * dvp
    rows = min(GLA_ROWS, seq)
    zg3 = zg.reshape(batch, seq, zg.shape[1])
    qk_blocks = 2 * pairs * dkp // dvp
    out = pl.pallas_call(
        _gla_kernel,
        grid=(batch, pairs, seq // rows),
        in_specs=[
            pl.BlockSpec((None, rows, dkp), lambda b, p, t: (b, t, p)),
            pl.BlockSpec((None, rows, dkp), lambda b, p, t: (b, t, pairs + p)),
            pl.BlockSpec((None, rows, dvp), lambda b, p, t: (b, t, qk_blocks + p)),
            pl.BlockSpec((None, rows, dvp), lambda b, p, t: (b, t, qk_blocks + pairs + p)),
            pl.BlockSpec((None, rows, LANES), lambda b, p, t: (b, t, (2 * pairs * dkp + 2 * width) // LANES)),
            pl.BlockSpec((None, LANES, dkp), lambda b, p, t: (l, 0, p)),
            pl.BlockSpec((None, 1, dkp), lambda b, p, t: (l, 0, p)),
            pl.BlockSpec((None, 1, HEAD_DIM), lambda b, p, t: (l, 0, 0)),
        ],
        out_specs=pl.BlockSpec((None, rows, dvp), lambda b, p, t: (b, t, p)),
        out_shape=jax.ShapeDtypeStruct((batch, seq, width), BF16),
        scratch_shapes=[pltpu.VMEM((dvp, dkp), F32), pltpu.VMEM((rows, dvp), F32)],
        compiler_params=_params("parallel", "parallel", "arbitrary"),
        name="gated_linear_attention",
    )(zg3, zg3, zg3, zg3, zg3, w_gate, b_gate, out_gain)
    return out.reshape(n, width)


def _outproj_kernel(x_ref, ya_ref, yb_ref, yc_ref, w_ref, o_ref):
    acc = x_ref[...]
    c = 0
    for y_ref in (ya_ref, yb_ref, yc_ref):
        wdt = y_ref.shape[1]
        acc = acc + _dot(y_ref[...], w_ref[c:c + wdt, :])
        c += wdt
    o_ref[...] = acc


def _outproj(x, ys, w_out, l):
    n, d = x.shape
    tm = min(PROJ_TM, n)
    return pl.pallas_call(
        _outproj_kernel,
        grid=(n // tm,),
        in_specs=[pl.BlockSpec((tm, d), lambda i: (i, 0))]
        + [pl.BlockSpec((tm, y.shape[1]), lambda i: (i, 0)) for y in ys]
        + [pl.BlockSpec((None, w_out.shape[1], d), lambda i: (l, 0, 0), pipeline_mode=pl.Buffered(1))],
        out_specs=pl.BlockSpec((tm, d), lambda i: (i, 0)),
        out_shape=jax.ShapeDtypeStruct((n, d), F32),
        compiler_params=_params("parallel"),
        name="out_proj",
    )(x, *ys, w_out)


def _rope_tables(seq):
    pos = jnp.arange(seq, dtype=F32)
    inv_freq = ROPE_THETA ** (-jnp.arange(0, ROPE_DIMS, 2, dtype=F32) / ROPE_DIMS)
    ang = pos[:, None] * inv_freq[None, :]
    cos, sin = jnp.cos(ang), jnp.sin(ang)
    rest = HEAD_DIM - ROPE_DIMS
    cos_t = jnp.concatenate([cos, cos, jnp.ones((seq, rest), F32)], axis=1)
    sin_t = jnp.concatenate([-sin, sin, jnp.zeros((seq, rest), F32)], axis=1)
    return cos_t, sin_t


def kernel(x, ffn_norm, ffn_w_gate, ffn_w_up, ffn_w_down, mix_norm, w_in, sg_norm, sg_w, sg_b, q_norm, k_norm, gla_w_gate, gla_b_gate, gla_out_norm, w_out):
    batch, seq, d = x.shape
    depth = w_in.shape[0]
    n = batch * seq
    sg_width = sg_norm.shape[-1]
    gla_qk = gla_w_gate.shape[-1]
    gla_width = (gla_qk // GLA_DK) * HEAD_DIM
    att_width = d - sg_width - gla_width
    n_in = w_in.shape[-1]
    assert n_in == 2 * sg_width + 3 * att_width + 2 * gla_qk + 2 * gla_width + GLA_GATE_RANK
    widths = (sg_width, 3 * att_width, 2 * gla_qk + 2 * gla_width + LANES)

    wg, wu, wd = ffn_w_gate.astype(BF16), ffn_w_up.astype(BF16), ffn_w_down.astype(BF16)
    w_in_p = w_in.astype(BF16)
    w_out_b = w_out.astype(BF16)
    gla_wg = jnp.pad(gla_w_gate, ((0, 0), (0, LANES - GLA_GATE_RANK), (0, 0))).astype(BF16)
    ffn_g = ffn_norm.reshape(depth, 2, 1, d)
    mix_g = mix_norm.reshape(depth, 1, d)
    sg_g = sg_norm.reshape(depth, 1, sg_width)
    sg_bt = jnp.swapaxes(sg_b, 1, 2)
    q_g = q_norm.reshape(depth, 1, HEAD_DIM)
    k_g = k_norm.reshape(depth, 1, HEAD_DIM)
    gla_bg = gla_b_gate.reshape(depth, 1, gla_qk)
    gla_og = gla_out_norm.reshape(depth, 1, HEAD_DIM)
    cos_t, sin_t = _rope_tables(seq)

    h = x.reshape(n, d)
    for l in range(depth):
        h = _ffn(h, ffn_g, wg, wu, wd, l, 0)
        ya, za, zg = _inproj(h, mix_g, w_in_p, cos_t, sin_t, q_g, k_g, sg_g, sg_w, sg_bt, l, widths)
        yb = _attention(za, batch)
        yc = _gla(zg, gla_wg, gla_bg, gla_og, l, batch)
        h = _outproj(h, (ya, yb, yc), w_out_b, l)
        h = _ffn(h, ffn_g, wg, wu, wd, l, 1)
    return h.reshape(batch, seq, d)
```

```python
import functools

import jax
import jax.numpy as jnp
from jax import lax
from jax.experimental import pallas as pl
from jax.experimental.pallas import tpu as pltpu

F32 = jnp.float32
BF16 = jnp.bfloat16

EPS = 1e-6
HEAD_DIM = 128
LANES = 128
SG_CHUNK = 128
ATT_DILATIONS = (1, 4, 16)
ATT_STEPS = 128
ATT_RESIDUES = 16
ROPE_THETA = 500000.0
ROPE_DIMS = HEAD_DIM // 4
GLA_DK = 64
GLA_CHUNK = 64
GLA_GATE_RANK = 16
GLA_GATE_NORMALIZER = 16.0
MASK_VALUE = -1e30

VMEM_LIMIT_BYTES = 56 * 1024 * 1024

FFN_TM, FFN_TF = 1024, 512
FFN_NORM_CHUNKS = 4
FFN_CAST_ROWS = 16
PROJ_TM = 512
GLA_ROWS = 2048
ATT_PREP_ROWS = 256
ATT_GROUP = 8
LOG2_E = 1.4426950408889634


def _params(*semantics):
    return pltpu.CompilerParams(dimension_semantics=semantics, vmem_limit_bytes=VMEM_LIMIT_BYTES)


def _rms_norm(x, g):
    return x * lax.rsqrt(jnp.mean(x * x, axis=-1, keepdims=True) + EPS) * g


def _dot(a, b):
    return jnp.dot(a, b, preferred_element_type=F32)


def _dot_nt(a, b):
    return lax.dot_general(a, b, (((1,), (1,)), ((), ())), preferred_element_type=F32)


def _dot_tn(a, b):
    return lax.dot_general(a, b, (((0,), (0,)), ((), ())), preferred_element_type=F32)


def _ffn_kernel(x_ref, g_ref, wg_ref, wu_ref, wd_ref, *rest):
    n_next = (len(rest) - 2) // 2
    next_f32, o_ref, next_bf16, h_ref = rest[:n_next], rest[n_next], rest[n_next + 1:-1], rest[-1]

    def cast_next():
        for src, dst in zip(next_f32, next_bf16):
            dst[...] = src[...].astype(BF16)

    first = pl.program_id(1) == 0

    def half_swiglu(h):
        gate = _dot(h, wg_ref[...])
        up = _dot(h, wu_ref[...])
        return ((gate * jax.nn.sigmoid(gate)) * (0.5 * up)).astype(BF16)

    @pl.when(first)
    def _():
        cast_next()
        rows = x_ref.shape[0] // FFN_NORM_CHUNKS
        acts = []
        for c in range(FFN_NORM_CHUNKS):
            rs = slice(c * rows, (c + 1) * rows)
            h = _rms_norm(x_ref[rs, :], g_ref[...]).astype(BF16)
            h_ref[rs, :] = h
            acts.append(half_swiglu(h))
        for c, act in enumerate(acts):
            rs = slice(c * rows, (c + 1) * rows)
            o_ref[rs, :] = x_ref[rs, :] + _dot(act, wd_ref[...])

    @pl.when(jnp.logical_not(first))
    def _():
        cast_next()
        o_ref[...] += _dot(half_swiglu(h_ref[...]), wd_ref[...])


def _ffn(x, norm_g, weights, stacked_f32, nxt, l, j):
    n, d = x.shape
    w_gate, w_up, w_down = weights
    f = w_gate.shape[-1]
    tm, tf = min(FFN_TM, n), FFN_TF
    grid = (n // tm, f // tf)
    steps = grid[0] * grid[1]
    slab = FFN_CAST_ROWS
    src = [w.reshape(w.shape[:2] + (steps * slab, -1)) for w in stacked_f32] if nxt else []
    src_specs = [pl.BlockSpec((None, None, slab, w.shape[-1]), lambda i, k: nxt + (i * grid[1] + k, 0)) for w in src]
    dst_specs = [pl.BlockSpec((slab, w.shape[-1]), lambda i, k: (i * grid[1] + k, 0)) for w in src]
    out = pl.pallas_call(
        _ffn_kernel,
        grid=grid,
        in_specs=[
            pl.BlockSpec((tm, d), lambda i, k: (i, 0)),
            pl.BlockSpec((None, None, 1, d), lambda i, k: (l, j, 0, 0)),
            pl.BlockSpec((d, tf), lambda i, k: (0, k)),
            pl.BlockSpec((d, tf), lambda i, k: (0, k)),
            pl.BlockSpec((tf, d), lambda i, k: (k, 0)),
        ] + src_specs,
        out_specs=[pl.BlockSpec((tm, d), lambda i, k: (i, 0))] + dst_specs,
        out_shape=[jax.ShapeDtypeStruct((n, d), F32)] + [jax.ShapeDtypeStruct(w.shape[2:], BF16) for w in src],
        scratch_shapes=[pltpu.VMEM((tm, d), BF16)],
        compiler_params=_params("parallel", "arbitrary"),
        name="ffn",
    )(x, norm_g, w_gate, w_up, w_down, *src)
    return out[0], tuple(c.reshape(w.shape[2:]) for c, w in zip(out[1:], stacked_f32))


def _inproj_kernel(x_ref, g_ref, w_ref, cos_ref, sin_ref, qg_ref, kg_ref, sgn_ref, sgw_ref, sgbt_ref,
                   *rest):
    if len(rest) == 5:
        rest[4][...] = rest[0][...].astype(BF16)
    ya_ref, za_ref, zg_ref = rest[-3:] if len(rest) == 3 else rest[1:4]
    h = _rms_norm(x_ref[...], g_ref[...]).astype(BF16)
    rows, sgw = ya_ref.shape
    c0 = 2 * sgw
    c1 = c0 + za_ref.shape[1]
    zs = _dot(h, w_ref[:, :c0])
    za = _dot(h, w_ref[:, c0:c1])
    c2 = c1 + zg_ref.shape[1] - LANES
    zg_ref[:, :c2 - c1] = _dot(h, w_ref[:, c1:c2]).astype(BF16)
    zg_ref[:, c2 - c1:] = jnp.zeros((rows, LANES), BF16)
    zg_ref[:, c2 - c1:c2 - c1 + GLA_GATE_RANK] = _dot(h, w_ref[:, c2:]).astype(BF16)

    heads = za_ref.shape[1] // (3 * HEAD_DIM)
    lane = lax.broadcasted_iota(jnp.int32, (1, HEAD_DIM), 1)
    half = ROPE_DIMS // 2
    cos = cos_ref[...]
    sin = sin_ref[...]
    for j in range(2 * heads):
        cols = slice(j * HEAD_DIM, (j + 1) * HEAD_DIM)
        gain, scale = (qg_ref, HEAD_DIM ** -0.5 * LOG2_E) if j < heads else (kg_ref, None)
        t = _rms_norm(za[:, cols], gain[...])
        partner = jnp.where(lane < half, pltpu.roll(t, HEAD_DIM - half, 1), pltpu.roll(t, half, 1))
        t = t * cos + partner * sin
        if scale is not None:
            t = t * scale
        za_ref[:, cols] = t.astype(BF16)
    za_ref[:, 2 * heads * HEAD_DIM:] = za[:, 2 * heads * HEAD_DIM:].astype(BF16)

    t = SG_CHUNK
    chunks = rows // t
    causal = lax.broadcasted_iota(jnp.int32, (t, t), 1) <= lax.broadcasted_iota(jnp.int32, (t, t), 0)
    for g in range(sgw // HEAD_DIM):
        cu = slice(g * HEAD_DIM, (g + 1) * HEAD_DIM)
        cv = slice(sgw + g * HEAD_DIM, sgw + (g + 1) * HEAD_DIM)
        v = _rms_norm(jax.nn.gelu(zs[:, cv]), sgn_ref[:, cu]).astype(BF16)
        v = jnp.concatenate([v[c * t:(c + 1) * t] for c in range(chunks)], axis=1)
        sv = _dot(jnp.where(causal, sgw_ref[g], 0.0).astype(BF16), v) + sgbt_ref[:, g:g + 1]
        sv = jnp.concatenate([sv[:, c * HEAD_DIM:(c + 1) * HEAD_DIM] for c in range(chunks)], axis=0)
        ya_ref[:, cu] = (jax.nn.gelu(zs[:, cu]) * sv).astype(BF16)


def _inproj(x, norm_g, w_in, w_in_f32, cos_t, sin_t, q_gain, k_gain, sg_norm, sg_w, sg_bt, l, widths):
    n, d = x.shape
    c = w_in.shape[-1]
    seq = cos_t.shape[0]
    sgw = sg_norm.shape[-1]
    groups = sgw // HEAD_DIM
    tm = min(PROJ_TM, seq)
    assert seq % tm == 0 and tm % SG_CHUNK == 0
    steps = n // tm
    cast_next = l + 1 < w_in_f32.shape[0]
    slab = d // steps
    assert not cast_next or (d % steps == 0 and slab % 16 == 0)

    def layer(*block):
        return pl.BlockSpec((None,) + block, lambda i: (l,) + (0,) * len(block))

    outs = pl.pallas_call(
        _inproj_kernel,
        grid=(steps,),
        in_specs=[
            pl.BlockSpec((tm, d), lambda i: (i, 0)),
            layer(1, d),
            pl.BlockSpec((d, c), lambda i: (0, 0), pipeline_mode=pl.Buffered(1)),
            pl.BlockSpec((tm, HEAD_DIM), lambda i: (i % (seq // tm), 0)),
            pl.BlockSpec((tm, HEAD_DIM), lambda i: (i % (seq // tm), 0)),
            layer(1, HEAD_DIM),
            layer(1, HEAD_DIM),
            layer(1, sgw),
            layer(groups, SG_CHUNK, SG_CHUNK),
            layer(SG_CHUNK, groups),
        ] + ([pl.BlockSpec((None, slab, c), lambda i: (l + 1, i, 0))] if cast_next else []),
        out_specs=[pl.BlockSpec((tm, w), lambda i: (i, 0)) for w in widths]
        + ([pl.BlockSpec((slab, c), lambda i: (i, 0))] if cast_next else []),
        out_shape=[jax.ShapeDtypeStruct((n, w), BF16) for w in widths]
        + ([jax.ShapeDtypeStruct((d, c), BF16)] if cast_next else []),
        compiler_params=_params("parallel"),
        name="in_proj",
    )(x, norm_g, w_in, cos_t, sin_t, q_gain, k_gain, sg_norm, sg_w, sg_bt, *([w_in_f32] if cast_next else []))
    return tuple(outs) + (() if cast_next else (None,))


def _att_kernel(qn, kn, v_ref, o_ref, qd, kd, vd, acc, m_s, l_s, s_buf, mb_buf):
    seq = qn.shape[0]
    w = ATT_STEPS
    res = ATT_RESIDUES
    per = seq // res
    pitch = acc.shape[0] // res
    pr = min(ATT_PREP_ROWS, seq)

    def stage(src, dst):
        def body(c, carry):
            t = src[pl.ds(pl.multiple_of(c * pr, pr), pr), :].astype(F32)
            for j in range(pr // res):
                acc[pl.ds(c * (pr // res) + j, res, stride=pitch), :] = t[j * res:(j + 1) * res, :]
            return carry

        lax.fori_loop(0, seq // pr, body, 0, unroll=2)

        def repack(r, carry):
            dst[r] = acc[pl.ds(pl.multiple_of(r * pitch, 8), per), :].astype(BF16)
            return carry

        lax.fori_loop(0, res, repack, 0)

    stage(qn, qd)
    stage(kn, kd)
    stage(v_ref, vd)

    row = lax.broadcasted_iota(jnp.int32, (w, w), 0)
    col = lax.broadcasted_iota(jnp.int32, (w, w), 1)
    group = s_buf.shape[0]

    def run_branch(load_qk, load_v, load_state, store, order_diff):
        def scores(g):
            for u in range(group):
                q, k2, has_prev = load_qk(g * group + u)
                s = _dot_nt(q, k2)
                prev_min = jnp.where(has_prev, 0, 2 * w)
                s = jnp.concatenate([jnp.where(order_diff >= prev_min, s[:, :w], MASK_VALUE),
                                     jnp.where(order_diff <= 0, s[:, w:], MASK_VALUE)], axis=1)
                s_buf[u] = s
                mb_buf[u] = jnp.broadcast_to(jnp.max(s, axis=1, keepdims=True), (w, HEAD_DIM))

        def consume(g):
            for u in range(group):
                i = g * group + u
                old = load_state(i)
                m_blk = mb_buf[u]
                m_new = m_blk if old is None else jnp.maximum(old[0], m_blk)
                p = jnp.exp2(s_buf[u] - jnp.concatenate([m_new, m_new], axis=1))
                p_sum = jnp.sum(p, axis=1, keepdims=True)
                pv = _dot(p.astype(BF16), load_v(i))
                if old is None:
                    store(i, m_new, jnp.broadcast_to(p_sum, (w, HEAD_DIM)), pv)
                else:
                    alpha = jnp.exp2(old[0] - m_new)
                    store(i, m_new, alpha * old[1] + p_sum, alpha * old[2] + pv)

        def body(g, carry):
            consume(g - 1)
            scores(g)
            return carry

        groups = seq // (w * group)
        scores(0)
        lax.fori_loop(1, groups, body, 0)
        consume(groups - 1)

    for d in ATT_DILATIONS[:0:-1]:
        first = d == ATT_DILATIONS[-1]
        pieces = res // d
        plen = w // pieces
        nb = per // plen
        order = lambda x: pieces * (x % plen) + x // plen

        def gather(ref, r, at, d=d, pieces=pieces, plen=plen):
            return jnp.concatenate([ref[c * d + r, pl.ds(at, plen), :] for c in range(pieces)], axis=0)

        def starts(i, nb=nb, plen=plen):
            n = i % nb
            return (i // nb, n, pl.multiple_of(n * plen, plen),
                    pl.multiple_of(jnp.where(n > 0, n - 1, n) * plen, plen))

        def state_rows(r, c, start, d=d, plen=plen):
            return pl.ds(pl.multiple_of((c * d + r) * pitch + start, 8), plen)

        def load_qk(i, gather=gather, starts=starts):
            r, n, start, pstart = starts(i)
            return gather(qd, r, start), jnp.concatenate([gather(kd, r, pstart), gather(kd, r, start)], axis=0), n > 0

        def load_v(i, gather=gather, starts=starts):
            r, _, start, pstart = starts(i)
            return jnp.concatenate([gather(vd, r, pstart), gather(vd, r, start)], axis=0)

        def load_state(i, first=first, pieces=pieces, starts=starts, state_rows=state_rows):
            if first:
                return None
            r, _, start, _ = starts(i)
            return tuple(jnp.concatenate([ref[state_rows(r, c, start), :] for c in range(pieces)], axis=0)
                         for ref in (m_s, l_s, acc))

        def store(i, *new, pieces=pieces, plen=plen, starts=starts, state_rows=state_rows):
            r, _, start, _ = starts(i)
            for ref, val in zip((m_s, l_s, acc), new):
                for c in range(pieces):
                    ref[state_rows(r, c, start), :] = val[c * plen:(c + 1) * plen, :]

        run_branch(load_qk, load_v, load_state, store, order(col) - order(row))

    def nat_starts(n):
        return pl.multiple_of(n * w, w), pl.multiple_of(jnp.where(n > 0, n - 1, n) * w, w)

    def nat_qk(n):
        start, pstart = nat_starts(n)
        return (qn[pl.ds(start, w), :],
                jnp.concatenate([kn[pl.ds(pstart, w), :], kn[pl.ds(start, w), :]], axis=0), n > 0)

    def nat_v(n):
        start, pstart = nat_starts(n)
        return jnp.concatenate([v_ref[pl.ds(pstart, w), :], v_ref[pl.ds(start, w), :]], axis=0)

    def nat_state(n):
        return tuple(jnp.concatenate([ref[pl.ds(n * (w // res) + j, res, stride=pitch), :]
                                      for j in range(w // res)], axis=0) for ref in (m_s, l_s, acc))

    def emit(n, m_new, l_new, acc_new):
        o_ref[pl.ds(nat_starts(n)[0], w), :] = (acc_new / l_new).astype(BF16)

    run_branch(nat_qk, nat_v, nat_state, emit, col - row)


def _attention(za, batch):
    n, w3 = za.shape
    seq = n // batch
    width = w3 // 3
    heads = width // HEAD_DIM
    assert seq % (ATT_STEPS * ATT_RESIDUES) == 0
    per = seq // ATT_RESIDUES
    pitch = per + 8
    za3 = za.reshape(batch, seq, w3)
    blk = (None, seq, HEAD_DIM)
    out = pl.pallas_call(
        _att_kernel,
        grid=(batch, heads),
        in_specs=[
            pl.BlockSpec(blk, lambda b, h: (b, 0, h)),
            pl.BlockSpec(blk, lambda b, h: (b, 0, heads + h)),
            pl.BlockSpec(blk, lambda b, h: (b, 0, 2 * heads + h)),
        ],
        out_specs=pl.BlockSpec(blk, lambda b, h: (b, 0, h)),
        out_shape=jax.ShapeDtypeStruct((batch, seq, width), BF16),
        scratch_shapes=[pltpu.VMEM((ATT_RESIDUES, per, HEAD_DIM), BF16) for _ in range(3)]
        + [pltpu.VMEM((ATT_RESIDUES * pitch, HEAD_DIM), F32) for _ in range(3)]
        + [pltpu.VMEM((ATT_GROUP, ATT_STEPS, 2 * ATT_STEPS), F32), pltpu.VMEM((ATT_GROUP, ATT_STEPS, HEAD_DIM), F32)],
        compiler_params=_params("parallel", "parallel"),
        name="dilated_attention",
    )(za3, za3, za3)
    return out.reshape(n, width)


def _gla_kernel(q_ref, k_ref, v_ref, g_ref, r_ref, wg_ref, bg_ref, og_ref, o_ref, st_ref, o_acc):
    rows = q_ref.shape[0]
    ck = GLA_CHUNK
    dv = HEAD_DIM

    @pl.when(pl.program_id(2) == 0)
    def _():
        st_ref[...] = jnp.zeros_like(st_ref)

    x = _dot(r_ref[...], wg_ref[...]) + bg_ref[...]
    log_a = (jnp.minimum(x, 0.0) - jnp.log(1.0 + jnp.exp(-jnp.abs(x)))) / GLA_GATE_NORMALIZER
    hi = log_a.astype(BF16)
    lo = (log_a - hi.astype(F32)).astype(BF16)
    grp = 4 * ck
    r_i = lax.broadcasted_iota(jnp.int32, (grp, grp), 0)
    c_i = lax.broadcasted_iota(jnp.int32, (grp, grp), 1)
    tri = jnp.where(((r_i // ck) == (c_i // ck)) & (c_i <= r_i), 1.0, 0.0).astype(BF16)
    b = jnp.concatenate([_dot(tri, hi[s * grp:(s + 1) * grp]) + _dot(tri, lo[s * grp:(s + 1) * grp])
                         for s in range(rows // grp)], axis=0)

    q = q_ref[...].astype(F32) * GLA_DK ** -0.5
    k = k_ref[...].astype(F32)
    q_dec = (q * jnp.exp(b)).astype(BF16)
    k_inv = (k * jnp.exp(-b)).astype(BF16)

    lane = lax.broadcasted_iota(jnp.int32, (1, 2 * GLA_DK), 1)
    head0 = lane < GLA_DK
    ar = lax.broadcasted_iota(jnp.int32, (ck, 2 * ck), 0)
    ac = lax.broadcasted_iota(jnp.int32, (ck, 2 * ck), 1)
    causal = (ac % ck) <= ar
    sr = lax.broadcasted_iota(jnp.int32, (2 * dv, 2 * GLA_DK), 0)
    sc = lax.broadcasted_iota(jnp.int32, (2 * dv, 2 * GLA_DK), 1)
    own = (sr < dv) == (sc < GLA_DK)
    zeros_v = jnp.zeros((ck, dv), BF16)

    chunks = [slice(c * ck, (c + 1) * ck) for c in range(rows // ck)]
    scores, incs, decays = [], [], []
    for rs in chunks:
        ki = k_inv[rs]
        ki2 = jnp.concatenate([jnp.where(head0, ki, 0), jnp.where(head0, 0, ki)], axis=0)
        scores.append(_dot_nt(q_dec[rs], ki2))
    for rs in chunks:
        b_last = b[rs.stop - 1:rs.stop, :]
        k_dec = (k[rs] * jnp.exp(b_last - b[rs])).astype(BF16)
        incs.append(_dot_tn(v_ref[rs, :], k_dec))
        decays.append(jnp.exp(b_last))
    for rs, a in zip(chunks, scores):
        vp = v_ref[rs, :]
        v_bd = jnp.concatenate(
            [jnp.concatenate([vp[:, :dv], zeros_v], axis=1),
             jnp.concatenate([zeros_v, vp[:, dv:]], axis=1)], axis=0)
        o_acc[rs, :] = _dot(jnp.where(causal, a, 0.0).astype(BF16), v_bd)
    state = st_ref[...]
    for rs, inc, decay in zip(chunks, incs, decays):
        o_acc[rs, :] += _dot_nt(q_dec[rs], state.astype(BF16))
        state = state * decay + jnp.where(own, inc, 0.0)
    st_ref[...] = state

    for h in range(2):
        cs = slice(h * dv, (h + 1) * dv)
        g = g_ref[:, cs].astype(F32)
        o = _rms_norm(o_acc[:, cs], og_ref[...]) * (g * jax.nn.sigmoid(g))
        o_ref[:, cs] = o.astype(BF16)


def _gla(zg, w_gate, b_gate, out_gain, l, batch):
    n, _ = zg.shape
    seq = n // batch
    dkp = 2 * GLA_DK
    dvp = 2 * HEAD_DIM
    pairs = w_gate.shape[-1] // dkp
    width = pairs * dvp
    rows = min(GLA_ROWS, seq)
    zg3 = zg.reshape(batch, seq, zg.shape[1])
    qk_blocks = 2 * pairs * dkp // dvp
    out = pl.pallas_call(
        _gla_kernel,
        grid=(batch, pairs, seq // rows),
        in_specs=[
            pl.BlockSpec((None, rows, dkp), lambda b, p, t: (b, t, p)),
            pl.BlockSpec((None, rows, dkp), lambda b, p, t: (b, t, pairs + p)),
            pl.BlockSpec((None, rows, dvp), lambda b, p, t: (b, t, qk_blocks + p)),
            pl.BlockSpec((None, rows, dvp), lambda b, p, t: (b, t, qk_blocks + pairs + p)),
            pl.BlockSpec((None, rows, LANES), lambda b, p, t: (b, t, (2 * pairs * dkp + 2 * width) // LANES)),
            pl.BlockSpec((None, LANES, dkp), lambda b, p, t: (l, 0, p)),
            pl.BlockSpec((None, 1, dkp), lambda b, p, t: (l, 0, p)),
            pl.BlockSpec((None, 1, HEAD_DIM), lambda b, p, t: (l, 0, 0)),
        ],
        out_specs=pl.BlockSpec((None, rows, dvp), lambda b, p, t: (b, t, p)),
        out_shape=jax.ShapeDtypeStruct((batch, seq, width), BF16),
        scratch_shapes=[pltpu.VMEM((dvp, dkp), F32), pltpu.VMEM((rows, dvp), F32)],
        compiler_params=_params("parallel", "parallel", "arbitrary"),
        name="gated_linear_attention",
    )(zg3, zg3, zg3, zg3, zg3, w_gate, b_gate, out_gain)
    return out.reshape(n, width)


def _outproj_kernel(x_ref, ya_ref, yb_ref, yc_ref, w_ref, o_ref):
    acc = x_ref[...]
    c = 0
    for y_ref in (ya_ref, yb_ref, yc_ref):
        wdt = y_ref.shape[1]
        acc = acc + _dot(y_ref[...], w_ref[c:c + wdt, :])
        c += wdt
    o_ref[...] = acc


def _outproj(x, ys, w_out, l):
    n, d = x.shape
    tm = min(PROJ_TM, n)
    return pl.pallas_call(
        _outproj_kernel,
        grid=(n // tm,),
        in_specs=[pl.BlockSpec((tm, d), lambda i: (i, 0))]
        + [pl.BlockSpec((tm, y.shape[1]), lambda i: (i, 0)) for y in ys]
        + [pl.BlockSpec((None, w_out.shape[1], d), lambda i: (l, 0, 0), pipeline_mode=pl.Buffered(1))],
        out_specs=pl.BlockSpec((tm, d), lambda i: (i, 0)),
        out_shape=jax.ShapeDtypeStruct((n, d), F32),
        compiler_params=_params("parallel"),
        name="out_proj",
    )(x, *ys, w_out)


def _rope_tables(seq):
    pos = jnp.arange(seq, dtype=F32)
    inv_freq = ROPE_THETA ** (-jnp.arange(0, ROPE_DIMS, 2, dtype=F32) / ROPE_DIMS)
    ang = pos[:, None] * inv_freq[None, :]
    cos, sin = jnp.cos(ang), jnp.sin(ang)
    rest = HEAD_DIM - ROPE_DIMS
    cos_t = jnp.concatenate([cos, cos, jnp.ones((seq, rest), F32)], axis=1)
    sin_t = jnp.concatenate([-sin, sin, jnp.zeros((seq, rest), F32)], axis=1)
    return cos_t, sin_t


def kernel(x, ffn_norm, ffn_w_gate, ffn_w_up, ffn_w_down, mix_norm, w_in, sg_norm, sg_w, sg_b, q_norm, k_norm, gla_w_gate, gla_b_gate, gla_out_norm, w_out):
    batch, seq, d = x.shape
    depth = w_in.shape[0]
    n = batch * seq
    sg_width = sg_norm.shape[-1]
    gla_qk = gla_w_gate.shape[-1]
    gla_width = (gla_qk // GLA_DK) * HEAD_DIM
    att_width = d - sg_width - gla_width
    n_in = w_in.shape[-1]
    assert n_in == 2 * sg_width + 3 * att_width + 2 * gla_qk + 2 * gla_width + GLA_GATE_RANK
    widths = (sg_width, 3 * att_width, 2 * gla_qk + 2 * gla_width + LANES)

    ffn_w = (ffn_w_gate, ffn_w_up, ffn_w_down)
    ffn_b = tuple(w[0, 0].astype(BF16) for w in ffn_w)
    w_in_b = w_in[0].astype(BF16)
    w_out_b = w_out.astype(BF16)
    gla_wg = jnp.pad(gla_w_gate, ((0, 0), (0, LANES - GLA_GATE_RANK), (0, 0))).astype(BF16)
    ffn_g = ffn_norm.reshape(depth, 2, 1, d)
    mix_g = mix_norm.reshape(depth, 1, d)
    sg_g = sg_norm.reshape(depth, 1, sg_width)
    sg_bt = jnp.swapaxes(sg_b, 1, 2)
    q_g = q_norm.reshape(depth, 1, HEAD_DIM)
    k_g = k_norm.reshape(depth, 1, HEAD_DIM)
    gla_bg = gla_b_gate.reshape(depth, 1, gla_qk)
    gla_og = gla_out_norm.reshape(depth, 1, HEAD_DIM)
    cos_t, sin_t = _rope_tables(seq)

    h = x.reshape(n, d)
    for l in range(depth):
        h, ffn_b = _ffn(h, ffn_g, ffn_b, ffn_w, (l, 1), l, 0)
        ya, za, zg, w_in_b = _inproj(h, mix_g, w_in_b, w_in, cos_t, sin_t, q_g, k_g, sg_g, sg_w, sg_bt, l, widths)
        yb = _attention(za, batch)
        yc = _gla(zg, gla_wg, gla_bg, gla_og, l, batch)
        h = _outproj(h, (ya, yb, yc), w_out_b, l)
        h, ffn_b = _ffn(h, ffn_g, ffn_b, ffn_w, (l + 1, 0) if l + 1 < depth else None, l, 1)
    return h.reshape(batch, seq, d)
```

```python
import functools

import jax
import jax.numpy as jnp
from jax import lax
from jax.experimental import pallas as pl
from jax.experimental.pallas import tpu as pltpu

F32 = jnp.float32
BF16 = jnp.bfloat16

EPS = 1e-6
HEAD_DIM = 128
LANES = 128
SG_CHUNK = 128
ATT_DILATIONS = (1, 4, 16)
ATT_STEPS = 128
ATT_RESIDUES = 16
ROPE_THETA = 500000.0
ROPE_DIMS = HEAD_DIM // 4
GLA_DK = 64
GLA_CHUNK = 64
GLA_GATE_RANK = 16
GLA_GATE_NORMALIZER = 16.0
MASK_VALUE = -1e30

VMEM_LIMIT_BYTES = 56 * 1024 * 1024

FFN_TM, FFN_TF = 1024, 512
FFN_NORM_CHUNKS = 4
PROJ_TM = 512
GLA_ROWS = 2048
ATT_PREP_ROWS = 256
ATT_GROUP = 8
LOG2_E = 1.4426950408889634


def _params(*semantics):
    return pltpu.CompilerParams(dimension_semantics=semantics, vmem_limit_bytes=VMEM_LIMIT_BYTES)


def _rms_norm(x, g):
    return x * lax.rsqrt(jnp.mean(x * x, axis=-1, keepdims=True) + EPS) * g


def _dot(a, b):
    return jnp.dot(a, b, preferred_element_type=F32)


def _dot_nt(a, b):
    return lax.dot_general(a, b, (((1,), (1,)), ((), ())), preferred_element_type=F32)


def _dot_tn(a, b):
    return lax.dot_general(a, b, (((0,), (0,)), ((), ())), preferred_element_type=F32)


def _ffn_kernel(x_ref, g_ref, wg_ref, wu_ref, wd_ref, *rest):
    n_next = (len(rest) - 2) // 2
    next_f32, o_ref, next_bf16, h_ref = rest[:n_next], rest[n_next], rest[n_next + 1:-1], rest[-1]

    def cast_next():
        for src, dst in zip(next_f32, next_bf16):
            dst[...] = src[...].astype(BF16)

    first = pl.program_id(1) == 0

    def half_swiglu(h):
        gate = _dot(h, wg_ref[...])
        up = _dot(h, wu_ref[...])
        return ((gate * jax.nn.sigmoid(gate)) * (0.5 * up)).astype(BF16)

    @pl.when(first)
    def _():
        cast_next()
        rows = x_ref.shape[0] // FFN_NORM_CHUNKS
        acts = []
        for c in range(FFN_NORM_CHUNKS):
            rs = slice(c * rows, (c + 1) * rows)
            h = _rms_norm(x_ref[rs, :], g_ref[...]).astype(BF16)
            h_ref[rs, :] = h
            acts.append(half_swiglu(h))
        for c, act in enumerate(acts):
            rs = slice(c * rows, (c + 1) * rows)
            o_ref[rs, :] = x_ref[rs, :] + _dot(act, wd_ref[...])

    @pl.when(jnp.logical_not(first))
    def _():
        cast_next()
        o_ref[...] += _dot(half_swiglu(h_ref[...]), wd_ref[...])


def _ffn(x, norm_g, weights, stacked_f32, nxt, l, j):
    n, d = x.shape
    w_gate, w_up, w_down = weights
    f = w_gate.shape[-1]
    tm, tf = min(FFN_TM, n), FFN_TF
    grid = (n // tm, f // tf)
    src = list(stacked_f32) if nxt else []
    up_rows, down_rows = d // grid[0], f // (grid[0] * grid[1])
    assert not nxt or (up_rows * grid[0] == d and down_rows * grid[0] * grid[1] == f
                       and up_rows % 16 == 0 and down_rows % 16 == 0)
    blocks = [(up_rows, tf), (up_rows, tf), (down_rows, d)][:len(src)]
    maps = [lambda i, k: (i, k), lambda i, k: (i, k), lambda i, k: (i * grid[1] + k, 0)][:len(src)]
    src_specs = [pl.BlockSpec((None, None) + b, lambda i, k, m=m: nxt + m(i, k)) for b, m in zip(blocks, maps)]
    dst_specs = [pl.BlockSpec(b, m) for b, m in zip(blocks, maps)]
    out = pl.pallas_call(
        _ffn_kernel,
        grid=grid,
        in_specs=[
            pl.BlockSpec((tm, d), lambda i, k: (i, 0)),
            pl.BlockSpec((None, None, 1, d), lambda i, k: (l, j, 0, 0)),
            pl.BlockSpec((d, tf), lambda i, k: (0, k)),
            pl.BlockSpec((d, tf), lambda i, k: (0, k)),
            pl.BlockSpec((tf, d), lambda i, k: (k, 0)),
        ] + src_specs,
        out_specs=[pl.BlockSpec((tm, d), lambda i, k: (i, 0))] + dst_specs,
        out_shape=[jax.ShapeDtypeStruct((n, d), F32)] + [jax.ShapeDtypeStruct(w.shape[2:], BF16) for w in src],
        scratch_shapes=[pltpu.VMEM((tm, d), BF16)],
        compiler_params=_params("parallel", "arbitrary"),
        name="ffn",
    )(x, norm_g, w_gate, w_up, w_down, *src)
    return out[0], tuple(out[1:])


def _inproj_kernel(x_ref, g_ref, w_ref, cos_ref, sin_ref, qg_ref, kg_ref, sgn_ref, sgw_ref, sgbt_ref,
                   *rest):
    if len(rest) == 5:
        rest[4][...] = rest[0][...].astype(BF16)
    ya_ref, za_ref, zg_ref = rest[-3:] if len(rest) == 3 else rest[1:4]
    h = _rms_norm(x_ref[...], g_ref[...]).astype(BF16)
    rows, sgw = ya_ref.shape
    c0 = 2 * sgw
    c1 = c0 + za_ref.shape[1]
    zs = _dot(h, w_ref[:, :c0])
    za = _dot(h, w_ref[:, c0:c1])
    c2 = c1 + zg_ref.shape[1] - LANES
    zg_ref[:, :c2 - c1] = _dot(h, w_ref[:, c1:c2]).astype(BF16)
    zg_ref[:, c2 - c1:] = jnp.zeros((rows, LANES), BF16)
    zg_ref[:, c2 - c1:c2 - c1 + GLA_GATE_RANK] = _dot(h, w_ref[:, c2:]).astype(BF16)

    heads = za_ref.shape[1] // (3 * HEAD_DIM)
    lane = lax.broadcasted_iota(jnp.int32, (1, HEAD_DIM), 1)
    half = ROPE_DIMS // 2
    cos = cos_ref[...]
    sin = sin_ref[...]
    for j in range(2 * heads):
        cols = slice(j * HEAD_DIM, (j + 1) * HEAD_DIM)
        gain, scale = (qg_ref, HEAD_DIM ** -0.5 * LOG2_E) if j < heads else (kg_ref, None)
        t = _rms_norm(za[:, cols], gain[...])
        partner = jnp.where(lane < half, pltpu.roll(t, HEAD_DIM - half, 1), pltpu.roll(t, half, 1))
        t = t * cos + partner * sin
        if scale is not None:
            t = t * scale
        za_ref[:, cols] = t.astype(BF16)
    za_ref[:, 2 * heads * HEAD_DIM:] = za[:, 2 * heads * HEAD_DIM:].astype(BF16)

    t = SG_CHUNK
    chunks = rows // t
    causal = lax.broadcasted_iota(jnp.int32, (t, t), 1) <= lax.broadcasted_iota(jnp.int32, (t, t), 0)
    for g in range(sgw // HEAD_DIM):
        cu = slice(g * HEAD_DIM, (g + 1) * HEAD_DIM)
        cv = slice(sgw + g * HEAD_DIM, sgw + (g + 1) * HEAD_DIM)
        v = _rms_norm(jax.nn.gelu(zs[:, cv]), sgn_ref[:, cu]).astype(BF16)
        v = jnp.concatenate([v[c * t:(c + 1) * t] for c in range(chunks)], axis=1)
        sv = _dot(jnp.where(causal, sgw_ref[g], 0.0).astype(BF16), v) + sgbt_ref[:, g:g + 1]
        sv = jnp.concatenate([sv[:, c * HEAD_DIM:(c + 1) * HEAD_DIM] for c in range(chunks)], axis=0)
        ya_ref[:, cu] = (jax.nn.gelu(zs[:, cu]) * sv).astype(BF16)


def _inproj(x, norm_g, w_in, w_in_f32, cos_t, sin_t, q_gain, k_gain, sg_norm, sg_w, sg_bt, l, widths):
    n, d = x.shape
    c = w_in.shape[-1]
    seq = cos_t.shape[0]
    sgw = sg_norm.shape[-1]
    groups = sgw // HEAD_DIM
    tm = min(PROJ_TM, seq)
    assert seq % tm == 0 and tm % SG_CHUNK == 0
    steps = n // tm
    cast_next = l + 1 < w_in_f32.shape[0]
    slab = d // steps
    assert not cast_next or (d % steps == 0 and slab % 16 == 0)

    def layer(*block):
        return pl.BlockSpec((None,) + block, lambda i: (l,) + (0,) * len(block))

    outs = pl.pallas_call(
        _inproj_kernel,
        grid=(steps,),
        in_specs=[
            pl.BlockSpec((tm, d), lambda i: (i, 0)),
            layer(1, d),
            pl.BlockSpec((d, c), lambda i: (0, 0), pipeline_mode=pl.Buffered(1)),
            pl.BlockSpec((tm, HEAD_DIM), lambda i: (i % (seq // tm), 0)),
            pl.BlockSpec((tm, HEAD_DIM), lambda i: (i % (seq // tm), 0)),
            layer(1, HEAD_DIM),
            layer(1, HEAD_DIM),
            layer(1, sgw),
            layer(groups, SG_CHUNK, SG_CHUNK),
            layer(SG_CHUNK, groups),
        ] + ([pl.BlockSpec((None, slab, c), lambda i: (l + 1, i, 0))] if cast_next else []),
        out_specs=[pl.BlockSpec((tm, w), lambda i: (i, 0)) for w in widths]
        + ([pl.BlockSpec((slab, c), lambda i: (i, 0))] if cast_next else []),
        out_shape=[jax.ShapeDtypeStruct((n, w), BF16) for w in widths]
        + ([jax.ShapeDtypeStruct((d, c), BF16)] if cast_next else []),
        compiler_params=_params("parallel"),
        name="in_proj",
    )(x, norm_g, w_in, cos_t, sin_t, q_gain, k_gain, sg_norm, sg_w, sg_bt, *([w_in_f32] if cast_next else []))
    return tuple(outs) + (() if cast_next else (None,))


def _att_kernel(qn, kn, v_ref, o_ref, qd, kd, vd, acc, m_s, l_s, s_buf, mb_buf):
    seq = qn.shape[0]
    w = ATT_STEPS
    res = ATT_RESIDUES
    per = seq // res
    pitch = acc.shape[0] // res
    pr = min(ATT_PREP_ROWS, seq)

    def stage(src, dst):
        def body(c, carry):
            t = src[pl.ds(pl.multiple_of(c * pr, pr), pr), :].astype(F32)
            for j in range(pr // res):
                acc[pl.ds(c * (pr // res) + j, res, stride=pitch), :] = t[j * res:(j + 1) * res, :]
            return carry

        lax.fori_loop(0, seq // pr, body, 0, unroll=2)

        def repack(r, carry):
            dst[r] = acc[pl.ds(pl.multiple_of(r * pitch, 8), per), :].astype(BF16)
            return carry

        lax.fori_loop(0, res, repack, 0)

    stage(qn, qd)
    stage(kn, kd)
    stage(v_ref, vd)

    row = lax.broadcasted_iota(jnp.int32, (w, w), 0)
    col = lax.broadcasted_iota(jnp.int32, (w, w), 1)
    group = s_buf.shape[0]

    def run_branch(load_qk, load_v, load_state, store, order_diff):
        def scores(g):
            for u in range(group):
                q, k2, has_prev = load_qk(g * group + u)
                s = _dot_nt(q, k2)
                prev_min = jnp.where(has_prev, 0, 2 * w)
                s = jnp.concatenate([jnp.where(order_diff >= prev_min, s[:, :w], MASK_VALUE),
                                     jnp.where(order_diff <= 0, s[:, w:], MASK_VALUE)], axis=1)
                s_buf[u] = s
                mb_buf[u] = jnp.broadcast_to(jnp.max(s, axis=1, keepdims=True), (w, HEAD_DIM))

        def consume(g):
            for u in range(group):
                i = g * group + u
                old = load_state(i)
                m_blk = mb_buf[u]
                m_new = m_blk if old is None else jnp.maximum(old[0], m_blk)
                p = jnp.exp2(s_buf[u] - jnp.concatenate([m_new, m_new], axis=1))
                p_sum = jnp.sum(p, axis=1, keepdims=True)
                pv = _dot(p.astype(BF16), load_v(i))
                if old is None:
                    store(i, m_new, jnp.broadcast_to(p_sum, (w, HEAD_DIM)), pv)
                else:
                    alpha = jnp.exp2(old[0] - m_new)
                    store(i, m_new, alpha * old[1] + p_sum, alpha * old[2] + pv)

        def body(g, carry):
            consume(g - 1)
            scores(g)
            return carry

        groups = seq // (w * group)
        scores(0)
        lax.fori_loop(1, groups, body, 0)
        consume(groups - 1)

    for d in ATT_DILATIONS[:0:-1]:
        first = d == ATT_DILATIONS[-1]
        pieces = res // d
        plen = w // pieces
        nb = per // plen
        order = lambda x: pieces * (x % plen) + x // plen

        def gather(ref, r, at, d=d, pieces=pieces, plen=plen):
            return jnp.concatenate([ref[c * d + r, pl.ds(at, plen), :] for c in range(pieces)], axis=0)

        def starts(i, nb=nb, plen=plen):
            n = i % nb
            return (i // nb, n, pl.multiple_of(n * plen, plen),
                    pl.multiple_of(jnp.where(n > 0, n - 1, n) * plen, plen))

        def state_rows(r, c, start, d=d, plen=plen):
            return pl.ds(pl.multiple_of((c * d + r) * pitch + start, 8), plen)

        def load_qk(i, gather=gather, starts=starts):
            r, n, start, pstart = starts(i)
            return gather(qd, r, start), jnp.concatenate([gather(kd, r, pstart), gather(kd, r, start)], axis=0), n > 0

        def load_v(i, gather=gather, starts=starts):
            r, _, start, pstart = starts(i)
            return jnp.concatenate([gather(vd, r, pstart), gather(vd, r, start)], axis=0)

        def load_state(i, first=first, pieces=pieces, starts=starts, state_rows=state_rows):
            if first:
                return None
            r, _, start, _ = starts(i)
            return tuple(jnp.concatenate([ref[state_rows(r, c, start), :] for c in range(pieces)], axis=0)
                         for ref in (m_s, l_s, acc))

        def store(i, *new, pieces=pieces, plen=plen, starts=starts, state_rows=state_rows):
            r, _, start, _ = starts(i)
            for ref, val in zip((m_s, l_s, acc), new):
                for c in range(pieces):
                    ref[state_rows(r, c, start), :] = val[c * plen:(c + 1) * plen, :]

        run_branch(load_qk, load_v, load_state, store, order(col) - order(row))

    def nat_starts(n):
        return pl.multiple_of(n * w, w), pl.multiple_of(jnp.where(n > 0, n - 1, n) * w, w)

    def nat_qk(n):
        start, pstart = nat_starts(n)
        return (qn[pl.ds(start, w), :],
                jnp.concatenate([kn[pl.ds(pstart, w), :], kn[pl.ds(start, w), :]], axis=0), n > 0)

    def nat_v(n):
        start, pstart = nat_starts(n)
        return jnp.concatenate([v_ref[pl.ds(pstart, w), :], v_ref[pl.ds(start, w), :]], axis=0)

    def nat_state(n):
        return tuple(jnp.concatenate([ref[pl.ds(n * (w // res) + j, res, stride=pitch), :]
                                      for j in range(w // res)], axis=0) for ref in (m_s, l_s, acc))

    def emit(n, m_new, l_new, acc_new):
        o_ref[pl.ds(nat_starts(n)[0], w), :] = (acc_new / l_new).astype(BF16)

    run_branch(nat_qk, nat_v, nat_state, emit, col - row)


def _attention(za, batch):
    n, w3 = za.shape
    seq = n // batch
    width = w3 // 3
    heads = width // HEAD_DIM
    assert seq % (ATT_STEPS * ATT_RESIDUES) == 0
    per = seq // ATT_RESIDUES
    pitch = per + 8
    za3 = za.reshape(batch, seq, w3)
    blk = (None, seq, HEAD_DIM)
    out = pl.pallas_call(
        _att_kernel,
        grid=(batch, heads),
        in_specs=[
            pl.BlockSpec(blk, lambda b, h: (b, 0, h)),
            pl.BlockSpec(blk, lambda b, h: (b, 0, heads + h)),
            pl.BlockSpec(blk, lambda b, h: (b, 0, 2 * heads + h)),
        ],
        out_specs=pl.BlockSpec(blk, lambda b, h: (b, 0, h)),
        out_shape=jax.ShapeDtypeStruct((batch, seq, width), BF16),
        scratch_shapes=[pltpu.VMEM((ATT_RESIDUES, per, HEAD_DIM), BF16) for _ in range(3)]
        + [pltpu.VMEM((ATT_RESIDUES * pitch, HEAD_DIM), F32) for _ in range(3)]
        + [pltpu.VMEM((ATT_GROUP, ATT_STEPS, 2 * ATT_STEPS), F32), pltpu.VMEM((ATT_GROUP, ATT_STEPS, HEAD_DIM), F32)],
        compiler_params=_params("parallel", "parallel"),
        name="dilated_attention",
    )(za3, za3, za3)
    return out.reshape(n, width)


def _gla_kernel(q_ref, k_ref, v_ref, g_ref, r_ref, wg_ref, bg_ref, og_ref, o_ref, st_ref, o_acc):
    rows = q_ref.shape[0]
    ck = GLA_CHUNK
    dv = HEAD_DIM

    @pl.when(pl.program_id(2) == 0)
    def _():
        st_ref[...] = jnp.zeros_like(st_ref)

    x = _dot(r_ref[...], wg_ref[...]) + bg_ref[...]
    log_a = (jnp.minimum(x, 0.0) - jnp.log(1.0 + jnp.exp(-jnp.abs(x)))) / GLA_GATE_NORMALIZER
    hi = log_a.astype(BF16)
    lo = (log_a - hi.astype(F32)).astype(BF16)
    grp = 4 * ck
    r_i = lax.broadcasted_iota(jnp.int32, (grp, grp), 0)
    c_i = lax.broadcasted_iota(jnp.int32, (grp, grp), 1)
    tri = jnp.where(((r_i // ck) == (c_i // ck)) & (c_i <= r_i), 1.0, 0.0).astype(BF16)
    b = jnp.concatenate([_dot(tri, hi[s * grp:(s + 1) * grp]) + _dot(tri, lo[s * grp:(s + 1) * grp])
                         for s in range(rows // grp)], axis=0)

    q = q_ref[...].astype(F32) * GLA_DK ** -0.5
    k = k_ref[...].astype(F32)
    q_dec = (q * jnp.exp(b)).astype(BF16)
    k_inv = (k * jnp.exp(-b)).astype(BF16)

    lane = lax.broadcasted_iota(jnp.int32, (1, 2 * GLA_DK), 1)
    head0 = lane < GLA_DK
    ar = lax.broadcasted_iota(jnp.int32, (ck, 2 * ck), 0)
    ac = lax.broadcasted_iota(jnp.int32, (ck, 2 * ck), 1)
    causal = (ac % ck) <= ar
    sr = lax.broadcasted_iota(jnp.int32, (2 * dv, 2 * GLA_DK), 0)
    sc = lax.broadcasted_iota(jnp.int32, (2 * dv, 2 * GLA_DK), 1)
    own = (sr < dv) == (sc < GLA_DK)
    zeros_v = jnp.zeros((ck, dv), BF16)

    chunks = [slice(c * ck, (c + 1) * ck) for c in range(rows // ck)]
    scores, incs, decays = [], [], []
    for rs in chunks:
        ki = k_inv[rs]
        ki2 = jnp.concatenate([jnp.where(head0, ki, 0), jnp.where(head0, 0, ki)], axis=0)
        scores.append(_dot_nt(q_dec[rs], ki2))
    for rs in chunks:
        b_last = b[rs.stop - 1:rs.stop, :]
        k_dec = (k[rs] * jnp.exp(b_last - b[rs])).astype(BF16)
        incs.append(_dot_tn(v_ref[rs, :], k_dec))
        decays.append(jnp.exp(b_last))
    for rs, a in zip(chunks, scores):
        vp = v_ref[rs, :]
        v_bd = jnp.concatenate(
            [jnp.concatenate([vp[:, :dv], zeros_v], axis=1),
             jnp.concatenate([zeros_v, vp[:, dv:]], axis=1)], axis=0)
        o_acc[rs, :] = _dot(jnp.where(causal, a, 0.0).astype(BF16), v_bd)
    state = st_ref[...]
    for rs, inc, decay in zip(chunks, incs, decays):
        o_acc[rs, :] += _dot_nt(q_dec[rs], state.astype(BF16))
        state = state * decay + jnp.where(own, inc, 0.0)
    st_ref[...] = state

    for h in range(2):
        cs = slice(h * dv, (h + 1) * dv)
        g = g_ref[:, cs].astype(F32)
        o = _rms_norm(o_acc[:, cs], og_ref[...]) * (g * jax.nn.sigmoid(g))
        o_ref[:, cs] = o.astype(BF16)


def _gla(zg, w_gate, b_gate, out_gain, l, batch):
    n, _ = zg.shape
    seq = n // batch
    dkp = 2 * GLA_DK
    dvp = 2 * HEAD_DIM
    pairs = w_gate.shape[-1] // dkp
    width = pairs * dvp
    rows = min(GLA_ROWS, seq)
    zg3 = zg.reshape(batch, seq, zg.shape[1])
    qk_blocks = 2 * pairs * dkp // dvp
    out = pl.pallas_call(
        _gla_kernel,
        grid=(batch, pairs, seq // rows),
        in_specs=[
            pl.BlockSpec((None, rows, dkp), lambda b, p, t: (b, t, p)),
            pl.BlockSpec((None, rows, dkp), lambda b, p, t: (b, t, pairs + p)),
            pl.BlockSpec((None, rows, dvp), lambda b, p, t: (b, t, qk_blocks + p)),
            pl.BlockSpec((None, rows, dvp), lambda b, p, t: (b, t, qk_blocks + pairs + p)),
            pl.BlockSpec((None, rows, LANES), lambda b, p, t: (b, t, (2 * pairs * dkp + 2 * width) // LANES)),
            pl.BlockSpec((None, LANES, dkp), lambda b, p, t: (l, 0, p)),
            pl.BlockSpec((None, 1, dkp), lambda b, p, t: (l, 0, p)),
            pl.BlockSpec((None, 1, HEAD_DIM), lambda b, p, t: (l, 0, 0)),
        ],
        out_specs=pl.BlockSpec((None, rows, dvp), lambda b, p, t: (b, t, p)),
        out_shape=jax.ShapeDtypeStruct((batch, seq, width), BF16),
        scratch_shapes=[pltpu.VMEM((dvp, dkp), F32), pltpu.VMEM((rows, dvp), F32)],
        compiler_params=_params("parallel", "parallel", "arbitrary"),
        name="gated_linear_attention",
    )(zg3, zg3, zg3, zg3, zg3, w_gate, b_gate, out_gain)
    return out.reshape(n, width)


def _outproj_kernel(x_ref, ya_ref, yb_ref, yc_ref, w_ref, o_ref):
    acc = x_ref[...]
    c = 0
    for y_ref in (ya_ref, yb_ref, yc_ref):
        wdt = y_ref.shape[1]
        acc = acc + _dot(y_ref[...], w_ref[c:c + wdt, :])
        c += wdt
    o_ref[...] = acc


def _outproj(x, ys, w_out, l):
    n, d = x.shape
    tm = min(PROJ_TM, n)
    return pl.pallas_call(
        _outproj_kernel,
        grid=(n // tm,),
        in_specs=[pl.BlockSpec((tm, d), lambda i: (i, 0))]
        + [pl.BlockSpec((tm, y.shape[1]), lambda i: (i, 0)) for y in ys]
        + [pl.BlockSpec((None, w_out.shape[1], d), lambda i: (l, 0, 0), pipeline_mode=pl.Buffered(1))],
        out_specs=pl.BlockSpec((tm, d), lambda i: (i, 0)),
        out_shape=jax.ShapeDtypeStruct((n, d), F32),
        compiler_params=_params("parallel"),
        name="out_proj",
    )(x, *ys, w_out)


def _rope_tables(seq):
    pos = jnp.arange(seq, dtype=F32)
    inv_freq = ROPE_THETA ** (-jnp.arange(0, ROPE_DIMS, 2, dtype=F32) / ROPE_DIMS)
    ang = pos[:, None] * inv_freq[None, :]
    cos, sin = jnp.cos(ang), jnp.sin(ang)
    rest = HEAD_DIM - ROPE_DIMS
    cos_t = jnp.concatenate([cos, cos, jnp.ones((seq, rest), F32)], axis=1)
    sin_t = jnp.concatenate([-sin, sin, jnp.zeros((seq, rest), F32)], axis=1)
    return cos_t, sin_t


def kernel(x, ffn_norm, ffn_w_gate, ffn_w_up, ffn_w_down, mix_norm, w_in, sg_norm, sg_w, sg_b, q_norm, k_norm, gla_w_gate, gla_b_gate, gla_out_norm, w_out):
    batch, seq, d = x.shape
    depth = w_in.shape[0]
    n = batch * seq
    sg_width = sg_norm.shape[-1]
    gla_qk = gla_w_gate.shape[-1]
    gla_width = (gla_qk // GLA_DK) * HEAD_DIM
    att_width = d - sg_width - gla_width
    n_in = w_in.shape[-1]
    assert n_in == 2 * sg_width + 3 * att_width + 2 * gla_qk + 2 * gla_width + GLA_GATE_RANK
    widths = (sg_width, 3 * att_width, 2 * gla_qk + 2 * gla_width + LANES)

    ffn_w = (ffn_w_gate, ffn_w_up, ffn_w_down)
    ffn_b = tuple(w[0, 0].astype(BF16) for w in ffn_w)
    w_in_b = w_in[0].astype(BF16)
    w_out_b = w_out.astype(BF16)
    gla_wg = jnp.pad(gla_w_gate, ((0, 0), (0, LANES - GLA_GATE_RANK), (0, 0))).astype(BF16)
    ffn_g = ffn_norm.reshape(depth, 2, 1, d)
    mix_g = mix_norm.reshape(depth, 1, d)
    sg_g = sg_norm.reshape(depth, 1, sg_width)
    sg_bt = jnp.swapaxes(sg_b, 1, 2)
    q_g = q_norm.reshape(depth, 1, HEAD_DIM)
    k_g = k_norm.reshape(depth, 1, HEAD_DIM)
    gla_bg = gla_b_gate.reshape(depth, 1, gla_qk)
    gla_og = gla_out_norm.reshape(depth, 1, HEAD_DIM)
    cos_t, sin_t = _rope_tables(seq)

    h = x.reshape(n, d)
    for l in range(depth):
        h, ffn_b = _ffn(h, ffn_g, ffn_b, ffn_w, (l, 1), l, 0)
        ya, za, zg, w_in_b = _inproj(h, mix_g, w_in_b, w_in, cos_t, sin_t, q_g, k_g, sg_g, sg_w, sg_bt, l, widths)
        yb = _attention(za, batch)
        yc = _gla(zg, gla_wg, gla_bg, gla_og, l, batch)
        h = _outproj(h, (ya, yb, yc), w_out_b, l)
        h, ffn_b = _ffn(h, ffn_g, ffn_b, ffn_w, (l + 1, 0) if l + 1 < depth else None, l, 1)
    return h.reshape(batch, seq, d)
```

```python
import functools

import jax
import jax.numpy as jnp
from jax import lax
from jax.experimental import pallas as pl
from jax.experimental.pallas import tpu as pltpu

F32 = jnp.float32
BF16 = jnp.bfloat16

EPS = 1e-6
HEAD_DIM = 128
LANES = 128
SG_CHUNK = 128
ATT_DILATIONS = (1, 4, 16)
ATT_STEPS = 128
ATT_RESIDUES = 16
ROPE_THETA = 500000.0
ROPE_DIMS = HEAD_DIM // 4
GLA_DK = 64
GLA_CHUNK = 64
GLA_GATE_RANK = 16
GLA_GATE_NORMALIZER = 16.0
MASK_VALUE = -1e30

VMEM_LIMIT_BYTES = 56 * 1024 * 1024

FFN_TM, FFN_TF = 1024, 512
FFN_NORM_CHUNKS = 4
PROJ_TM = 512
GLA_ROWS = 2048
ATT_GROUP = 8
LOG2_E = 1.4426950408889634


def _params(*semantics):
    return pltpu.CompilerParams(dimension_semantics=semantics, vmem_limit_bytes=VMEM_LIMIT_BYTES)


def _rms_norm(x, g):
    return x * lax.rsqrt(jnp.mean(x * x, axis=-1, keepdims=True) + EPS) * g


def _dot(a, b):
    return jnp.dot(a, b, preferred_element_type=F32)


def _dot_nt(a, b):
    return lax.dot_general(a, b, (((1,), (1,)), ((), ())), preferred_element_type=F32)


def _dot_tn(a, b):
    return lax.dot_general(a, b, (((0,), (0,)), ((), ())), preferred_element_type=F32)


def _ffn_kernel(x_ref, g_ref, wg_ref, wu_ref, wd_ref, *rest):
    n_next = (len(rest) - 2) // 2
    next_f32, o_ref, next_bf16, h_ref = rest[:n_next], rest[n_next], rest[n_next + 1:-1], rest[-1]

    def cast_next():
        for src, dst in zip(next_f32, next_bf16):
            dst[...] = src[...].astype(BF16)

    first = pl.program_id(1) == 0

    def half_swiglu(h):
        gate = _dot(h, wg_ref[...])
        up = _dot(h, wu_ref[...])
        return ((gate * jax.nn.sigmoid(gate)) * (0.5 * up)).astype(BF16)

    @pl.when(first)
    def _():
        cast_next()
        rows = x_ref.shape[0] // FFN_NORM_CHUNKS
        acts = []
        for c in range(FFN_NORM_CHUNKS):
            rs = slice(c * rows, (c + 1) * rows)
            h = _rms_norm(x_ref[rs, :], g_ref[...]).astype(BF16)
            h_ref[rs, :] = h
            acts.append(half_swiglu(h))
        for c, act in enumerate(acts):
            rs = slice(c * rows, (c + 1) * rows)
            o_ref[rs, :] = x_ref[rs, :] + _dot(act, wd_ref[...])

    @pl.when(jnp.logical_not(first))
    def _():
        cast_next()
        o_ref[...] += _dot(half_swiglu(h_ref[...]), wd_ref[...])


def _ffn(x, norm_g, weights, stacked_f32, nxt, l, j):
    n, d = x.shape
    w_gate, w_up, w_down = weights
    f = w_gate.shape[-1]
    tm, tf = min(FFN_TM, n), FFN_TF
    grid = (n // tm, f // tf)
    src = list(stacked_f32) if nxt else []
    up_rows, down_rows = d // grid[0], f // (grid[0] * grid[1])
    assert not nxt or (up_rows * grid[0] == d and down_rows * grid[0] * grid[1] == f
                       and up_rows % 16 == 0 and down_rows % 16 == 0)
    blocks = [(up_rows, tf), (up_rows, tf), (down_rows, d)][:len(src)]
    maps = [lambda i, k: (i, k), lambda i, k: (i, k), lambda i, k: (i * grid[1] + k, 0)][:len(src)]
    src_specs = [pl.BlockSpec((None, None) + b, lambda i, k, m=m: nxt + m(i, k)) for b, m in zip(blocks, maps)]
    dst_specs = [pl.BlockSpec(b, m) for b, m in zip(blocks, maps)]
    out = pl.pallas_call(
        _ffn_kernel,
        grid=grid,
        in_specs=[
            pl.BlockSpec((tm, d), lambda i, k: (i, 0)),
            pl.BlockSpec((None, None, 1, d), lambda i, k: (l, j, 0, 0)),
            pl.BlockSpec((d, tf), lambda i, k: (0, k)),
            pl.BlockSpec((d, tf), lambda i, k: (0, k)),
            pl.BlockSpec((tf, d), lambda i, k: (k, 0)),
        ] + src_specs,
        out_specs=[pl.BlockSpec((tm, d), lambda i, k: (i, 0))] + dst_specs,
        out_shape=[jax.ShapeDtypeStruct((n, d), F32)] + [jax.ShapeDtypeStruct(w.shape[2:], BF16) for w in src],
        scratch_shapes=[pltpu.VMEM((tm, d), BF16)],
        compiler_params=_params("parallel", "arbitrary"),
        name="ffn",
    )(x, norm_g, w_gate, w_up, w_down, *src)
    return out[0], tuple(out[1:])


def _inproj_kernel(x_ref, g_ref, w_ref, cos_ref, sin_ref, qg_ref, kg_ref, sgn_ref, sgw_ref, sgbt_ref,
                   ya_ref, za_ref, zg_ref, zd_ref, stage):
    h = _rms_norm(x_ref[...], g_ref[...]).astype(BF16)
    rows, sgw = ya_ref.shape
    c0 = 2 * sgw
    c1 = c0 + za_ref.shape[1]
    zs = _dot(h, w_ref[:, :c0])
    za = _dot(h, w_ref[:, c0:c1])
    c2 = c1 + zg_ref.shape[1] - LANES
    zg_ref[:, :c2 - c1] = _dot(h, w_ref[:, c1:c2]).astype(BF16)
    zg_ref[:, c2 - c1:] = jnp.zeros((rows, LANES), BF16)
    zg_ref[:, c2 - c1:c2 - c1 + GLA_GATE_RANK] = _dot(h, w_ref[:, c2:]).astype(BF16)

    heads = za_ref.shape[1] // (3 * HEAD_DIM)
    lane = lax.broadcasted_iota(jnp.int32, (1, HEAD_DIM), 1)
    half = ROPE_DIMS // 2
    cos = cos_ref[...]
    sin = sin_ref[...]
    res = zd_ref.shape[0]
    per = rows // res
    pitch = stage.shape[1] // res
    for j in range(3 * heads):
        cols = slice(j * HEAD_DIM, (j + 1) * HEAD_DIM)
        t = za[:, cols]
        if j < 2 * heads:
            gain, scale = (qg_ref, HEAD_DIM ** -0.5 * LOG2_E) if j < heads else (kg_ref, None)
            t = _rms_norm(t, gain[...])
            partner = jnp.where(lane < half, pltpu.roll(t, HEAD_DIM - half, 1), pltpu.roll(t, half, 1))
            t = t * cos + partner * sin
            if scale is not None:
                t = t * scale
        za_ref[:, cols] = t.astype(BF16)
        buf = stage.at[j % stage.shape[0]]
        for i in range(per):
            buf[pl.ds(i, res, stride=pitch), :] = t[i * res:(i + 1) * res, :]
        for r in range(res):
            zd_ref[r, :, cols] = buf[r * pitch:r * pitch + per, :].astype(BF16)

    t = SG_CHUNK
    chunks = rows // t
    causal = lax.broadcasted_iota(jnp.int32, (t, t), 1) <= lax.broadcasted_iota(jnp.int32, (t, t), 0)
    for g in range(sgw // HEAD_DIM):
        cu = slice(g * HEAD_DIM, (g + 1) * HEAD_DIM)
        cv = slice(sgw + g * HEAD_DIM, sgw + (g + 1) * HEAD_DIM)
        v = _rms_norm(jax.nn.gelu(zs[:, cv]), sgn_ref[:, cu]).astype(BF16)
        v = jnp.concatenate([v[c * t:(c + 1) * t] for c in range(chunks)], axis=1)
        sv = _dot(jnp.where(causal, sgw_ref[g], 0.0).astype(BF16), v) + sgbt_ref[:, g:g + 1]
        sv = jnp.concatenate([sv[:, c * HEAD_DIM:(c + 1) * HEAD_DIM] for c in range(chunks)], axis=0)
        ya_ref[:, cu] = (jax.nn.gelu(zs[:, cu]) * sv).astype(BF16)


def _inproj(x, norm_g, w_in, cos_t, sin_t, q_gain, k_gain, sg_norm, sg_w, sg_bt, l, widths):
    n, d = x.shape
    c = w_in.shape[-1]
    seq = cos_t.shape[0]
    batch = n // seq
    sgw = sg_norm.shape[-1]
    groups = sgw // HEAD_DIM
    res = ATT_RESIDUES
    tm = min(PROJ_TM, seq)
    tiles = seq // tm
    per = tm // res
    assert seq % tm == 0 and tm % SG_CHUNK == 0 and per % 16 == 0
    pitch = per + 8

    def layer(*block):
        return pl.BlockSpec((None,) + block, lambda i: (l,) + (0,) * len(block))

    return pl.pallas_call(
        _inproj_kernel,
        grid=(n // tm,),
        in_specs=[
            pl.BlockSpec((tm, d), lambda i: (i, 0)),
            layer(1, d),
            pl.BlockSpec((None, d, c), lambda i: (l, 0, 0), pipeline_mode=pl.Buffered(1)),
            pl.BlockSpec((tm, HEAD_DIM), lambda i: (i % tiles, 0)),
            pl.BlockSpec((tm, HEAD_DIM), lambda i: (i % tiles, 0)),
            layer(1, HEAD_DIM),
            layer(1, HEAD_DIM),
            layer(1, sgw),
            layer(groups, SG_CHUNK, SG_CHUNK),
            layer(SG_CHUNK, groups),
        ],
        out_specs=[pl.BlockSpec((tm, w), lambda i: (i, 0)) for w in widths]
        + [pl.BlockSpec((None, res, per, widths[1]), lambda i: (i // tiles, 0, i % tiles, 0))],
        out_shape=[jax.ShapeDtypeStruct((n, w), BF16) for w in widths]
        + [jax.ShapeDtypeStruct((batch, res, seq // res, widths[1]), BF16)],
        scratch_shapes=[pltpu.VMEM((2, res * pitch, HEAD_DIM), F32)],
        compiler_params=_params("parallel"),
        name="in_proj",
    )(x, norm_g, w_in, cos_t, sin_t, q_gain, k_gain, sg_norm, sg_w, sg_bt)


def _att_kernel(qn, kn, v_ref, qd, kd, vd, o_ref, acc, m_s, l_s, s_buf, mb_buf):
    seq = qn.shape[0]
    w = ATT_STEPS
    res = ATT_RESIDUES
    per = seq // res
    pitch = acc.shape[0] // res

    row = lax.broadcasted_iota(jnp.int32, (w, w), 0)
    col = lax.broadcasted_iota(jnp.int32, (w, w), 1)
    group = s_buf.shape[0]

    def run_branch(load_qk, load_v, load_state, store, order_diff):
        def scores(g):
            for u in range(group):
                q, k2, has_prev = load_qk(g * group + u)
                s = _dot_nt(q, k2)
                prev_min = jnp.where(has_prev, 0, 2 * w)
                s = jnp.concatenate([jnp.where(order_diff >= prev_min, s[:, :w], MASK_VALUE),
                                     jnp.where(order_diff <= 0, s[:, w:], MASK_VALUE)], axis=1)
                s_buf[u] = s
                mb_buf[u] = jnp.broadcast_to(jnp.max(s, axis=1, keepdims=True), (w, HEAD_DIM))

        def consume(g):
            for u in range(group):
                i = g * group + u
                old = load_state(i)
                m_blk = mb_buf[u]
                m_new = m_blk if old is None else jnp.maximum(old[0], m_blk)
                p = jnp.exp2(s_buf[u] - jnp.concatenate([m_new, m_new], axis=1))
                p_sum = jnp.sum(p, axis=1, keepdims=True)
                pv = _dot(p.astype(BF16), load_v(i))
                if old is None:
                    store(i, m_new, jnp.broadcast_to(p_sum, (w, HEAD_DIM)), pv)
                else:
                    alpha = jnp.exp2(old[0] - m_new)
                    store(i, m_new, alpha * old[1] + p_sum, alpha * old[2] + pv)

        def body(g, carry):
            consume(g - 1)
            scores(g)
            return carry

        groups = seq // (w * group)
        scores(0)
        lax.fori_loop(1, groups, body, 0)
        consume(groups - 1)

    for d in ATT_DILATIONS[:0:-1]:
        first = d == ATT_DILATIONS[-1]
        pieces = res // d
        plen = w // pieces
        nb = per // plen
        order = lambda x: pieces * (x % plen) + x // plen

        def gather(ref, r, at, d=d, pieces=pieces, plen=plen):
            return jnp.concatenate([ref[c * d + r, pl.ds(at, plen), :] for c in range(pieces)], axis=0)

        def starts(i, nb=nb, plen=plen):
            n = i % nb
            return (i // nb, n, pl.multiple_of(n * plen, plen),
                    pl.multiple_of(jnp.where(n > 0, n - 1, n) * plen, plen))

        def state_rows(r, c, start, d=d, plen=plen):
            return pl.ds(pl.multiple_of((c * d + r) * pitch + start, 8), plen)

        def load_qk(i, gather=gather, starts=starts):
            r, n, start, pstart = starts(i)
            return gather(qd, r, start), jnp.concatenate([gather(kd, r, pstart), gather(kd, r, start)], axis=0), n > 0

        def load_v(i, gather=gather, starts=starts):
            r, _, start, pstart = starts(i)
            return jnp.concatenate([gather(vd, r, pstart), gather(vd, r, start)], axis=0)

        def load_state(i, first=first, pieces=pieces, starts=starts, state_rows=state_rows):
            if first:
                return None
            r, _, start, _ = starts(i)
            return tuple(jnp.concatenate([ref[state_rows(r, c, start), :] for c in range(pieces)], axis=0)
                         for ref in (m_s, l_s, acc))

        def store(i, *new, pieces=pieces, plen=plen, starts=starts, state_rows=state_rows):
            r, _, start, _ = starts(i)
            for ref, val in zip((m_s, l_s, acc), new):
                for c in range(pieces):
                    ref[state_rows(r, c, start), :] = val[c * plen:(c + 1) * plen, :]

        run_branch(load_qk, load_v, load_state, store, order(col) - order(row))

    def nat_starts(n):
        return pl.multiple_of(n * w, w), pl.multiple_of(jnp.where(n > 0, n - 1, n) * w, w)

    def nat_qk(n):
        start, pstart = nat_starts(n)
        return (qn[pl.ds(start, w), :],
                jnp.concatenate([kn[pl.ds(pstart, w), :], kn[pl.ds(start, w), :]], axis=0), n > 0)

    def nat_v(n):
        start, pstart = nat_starts(n)
        return jnp.concatenate([v_ref[pl.ds(pstart, w), :], v_ref[pl.ds(start, w), :]], axis=0)

    def nat_state(n):
        return tuple(jnp.concatenate([ref[pl.ds(n * (w // res) + j, res, stride=pitch), :]
                                      for j in range(w // res)], axis=0) for ref in (m_s, l_s, acc))

    def emit(n, m_new, l_new, acc_new):
        o_ref[pl.ds(nat_starts(n)[0], w), :] = (acc_new / l_new).astype(BF16)

    run_branch(nat_qk, nat_v, nat_state, emit, col - row)


def _attention(za, zd):
    n, w3 = za.shape
    batch, res, per, _ = zd.shape
    seq = n // batch
    width = w3 // 3
    heads = width // HEAD_DIM
    assert res == ATT_RESIDUES and per * res == seq and seq % (ATT_STEPS * res) == 0
    pitch = per + 8
    za3 = za.reshape(batch, seq, w3)
    blk = (None, seq, HEAD_DIM)
    blk_d = (None, res, per, HEAD_DIM)
    out = pl.pallas_call(
        _att_kernel,
        grid=(batch, heads),
        in_specs=[
            pl.BlockSpec(blk, lambda b, h: (b, 0, h)),
            pl.BlockSpec(blk, lambda b, h: (b, 0, heads + h)),
            pl.BlockSpec(blk, lambda b, h: (b, 0, 2 * heads + h)),
            pl.BlockSpec(blk_d, lambda b, h: (b, 0, 0, h)),
            pl.BlockSpec(blk_d, lambda b, h: (b, 0, 0, heads + h)),
            pl.BlockSpec(blk_d, lambda b, h: (b, 0, 0, 2 * heads + h)),
        ],
        out_specs=pl.BlockSpec(blk, lambda b, h: (b, 0, h)),
        out_shape=jax.ShapeDtypeStruct((batch, seq, width), BF16),
        scratch_shapes=[pltpu.VMEM((res * pitch, HEAD_DIM), F32) for _ in range(3)]
        + [pltpu.VMEM((ATT_GROUP, ATT_STEPS, 2 * ATT_STEPS), F32), pltpu.VMEM((ATT_GROUP, ATT_STEPS, HEAD_DIM), F32)],
        compiler_params=_params("parallel", "parallel"),
        name="dilated_attention",
    )(za3, za3, za3, zd, zd, zd)
    return out.reshape(n, width)


def _gla_kernel(q_ref, k_ref, v_ref, g_ref, r_ref, wg_ref, bg_ref, og_ref, o_ref, st_ref, o_acc):
    rows = q_ref.shape[0]
    ck = GLA_CHUNK
    dv = HEAD_DIM

    @pl.when(pl.program_id(2) == 0)
    def _():
        st_ref[...] = jnp.zeros_like(st_ref)

    x = _dot(r_ref[...], wg_ref[...]) + bg_ref[...]
    log_a = (jnp.minimum(x, 0.0) - jnp.log(1.0 + jnp.exp(-jnp.abs(x)))) / GLA_GATE_NORMALIZER
    hi = log_a.astype(BF16)
    lo = (log_a - hi.astype(F32)).astype(BF16)
    grp = 4 * ck
    r_i = lax.broadcasted_iota(jnp.int32, (grp, grp), 0)
    c_i = lax.broadcasted_iota(jnp.int32, (grp, grp), 1)
    tri = jnp.where(((r_i // ck) == (c_i // ck)) & (c_i <= r_i), 1.0, 0.0).astype(BF16)
    b = jnp.concatenate([_dot(tri, hi[s * grp:(s + 1) * grp]) + _dot(tri, lo[s * grp:(s + 1) * grp])
                         for s in range(rows // grp)], axis=0)

    q = q_ref[...].astype(F32) * GLA_DK ** -0.5
    k = k_ref[...].astype(F32)
    q_dec = (q * jnp.exp(b)).astype(BF16)
    k_inv = (k * jnp.exp(-b)).astype(BF16)

    lane = lax.broadcasted_iota(jnp.int32, (1, 2 * GLA_DK), 1)
    head0 = lane < GLA_DK
    ar = lax.broadcasted_iota(jnp.int32, (ck, 2 * ck), 0)
    ac = lax.broadcasted_iota(jnp.int32, (ck, 2 * ck), 1)
    causal = (ac % ck) <= ar
    sr = lax.broadcasted_iota(jnp.int32, (2 * dv, 2 * GLA_DK), 0)
    sc = lax.broadcasted_iota(jnp.int32, (2 * dv, 2 * GLA_DK), 1)
    own = (sr < dv) == (sc < GLA_DK)
    zeros_v = jnp.zeros((ck, dv), BF16)

    chunks = [slice(c * ck, (c + 1) * ck) for c in range(rows // ck)]
    scores, incs, decays = [], [], []
    for rs in chunks:
        ki = k_inv[rs]
        ki2 = jnp.concatenate([jnp.where(head0, ki, 0), jnp.where(head0, 0, ki)], axis=0)
        scores.append(_dot_nt(q_dec[rs], ki2))
    for rs in chunks:
        b_last = b[rs.stop - 1:rs.stop, :]
        k_dec = (k[rs] * jnp.exp(b_last - b[rs])).astype(BF16)
        incs.append(_dot_tn(v_ref[rs, :], k_dec))
        decays.append(jnp.exp(b_last))
    for rs, a in zip(chunks, scores):
        vp = v_ref[rs, :]
        v_bd = jnp.concatenate(
            [jnp.concatenate([vp[:, :dv], zeros_v], axis=1),
             jnp.concatenate([zeros_v, vp[:, dv:]], axis=1)], axis=0)
        o_acc[rs, :] = _dot(jnp.where(causal, a, 0.0).astype(BF16), v_bd)
    state = st_ref[...]
    for rs, inc, decay in zip(chunks, incs, decays):
        o_acc[rs, :] += _dot_nt(q_dec[rs], state.astype(BF16))
        state = state * decay + jnp.where(own, inc, 0.0)
    st_ref[...] = state

    for h in range(2):
        cs = slice(h * dv, (h + 1) * dv)
        g = g_ref[:, cs].astype(F32)
        o = _rms_norm(o_acc[:, cs], og_ref[...]) * (g * jax.nn.sigmoid(g))
        o_ref[:, cs] = o.astype(BF16)


def _gla(zg, w_gate, b_gate, out_gain, l, batch):
    n, _ = zg.shape
    seq = n // batch
    dkp = 2 * GLA_DK
    dvp = 2 * HEAD_DIM
    pairs = w_gate.shape[-1] // dkp
    width = pairs * dvp
    rows = min(GLA_ROWS, seq)
    zg3 = zg.reshape(batch, seq, zg.shape[1])
    qk_blocks = 2 * pairs * dkp // dvp
    out = pl.pallas_call(
        _gla_kernel,
        grid=(batch, pairs, seq // rows),
        in_specs=[
            pl.BlockSpec((None, rows, dkp), lambda b, p, t: (b, t, p)),
            pl.BlockSpec((None, rows, dkp), lambda b, p, t: (b, t, pairs + p)),
            pl.BlockSpec((None, rows, dvp), lambda b, p, t: (b, t, qk_blocks + p)),
            pl.BlockSpec((None, rows, dvp), lambda b, p, t: (b, t, qk_blocks + pairs + p)),
            pl.BlockSpec((None, rows, LANES), lambda b, p, t: (b, t, (2 * pairs * dkp + 2 * width) // LANES)),
            pl.BlockSpec((None, LANES, dkp), lambda b, p, t: (l, 0, p)),
            pl.BlockSpec((None, 1, dkp), lambda b, p, t: (l, 0, p)),
            pl.BlockSpec((None, 1, HEAD_DIM), lambda b, p, t: (l, 0, 0)),
        ],
        out_specs=pl.BlockSpec((None, rows, dvp), lambda b, p, t: (b, t, p)),
        out_shape=jax.ShapeDtypeStruct((batch, seq, width), BF16),
        scratch_shapes=[pltpu.VMEM((dvp, dkp), F32), pltpu.VMEM((rows, dvp), F32)],
        compiler_params=_params("parallel", "parallel", "arbitrary"),
        name="gated_linear_attention",
    )(zg3, zg3, zg3, zg3, zg3, w_gate, b_gate, out_gain)
    return out.reshape(n, width)


def _outproj_kernel(x_ref, ya_ref, yb_ref, yc_ref, w_ref, o_ref):
    acc = x_ref[...]
    c = 0
    for y_ref in (ya_ref, yb_ref, yc_ref):
        wdt = y_ref.shape[1]
        acc = acc + _dot(y_ref[...], w_ref[c:c + wdt, :])
        c += wdt
    o_ref[...] = acc


def _outproj(x, ys, w_out, l):
    n, d = x.shape
    tm = min(PROJ_TM, n)
    return pl.pallas_call(
        _outproj_kernel,
        grid=(n // tm,),
        in_specs=[pl.BlockSpec((tm, d), lambda i: (i, 0))]
        + [pl.BlockSpec((tm, y.shape[1]), lambda i: (i, 0)) for y in ys]
        + [pl.BlockSpec((None, w_out.shape[1], d), lambda i: (l, 0, 0), pipeline_mode=pl.Buffered(1))],
        out_specs=pl.BlockSpec((tm, d), lambda i: (i, 0)),
        out_shape=jax.ShapeDtypeStruct((n, d), F32),
        compiler_params=_params("parallel"),
        name="out_proj",
    )(x, *ys, w_out)


def _rope_tables(seq):
    pos = jnp.arange(seq, dtype=F32)
    inv_freq = ROPE_THETA ** (-jnp.arange(0, ROPE_DIMS, 2, dtype=F32) / ROPE_DIMS)
    ang = pos[:, None] * inv_freq[None, :]
    cos, sin = jnp.cos(ang), jnp.sin(ang)
    rest = HEAD_DIM - ROPE_DIMS
    cos_t = jnp.concatenate([cos, cos, jnp.ones((seq, rest), F32)], axis=1)
    sin_t = jnp.concatenate([-sin, sin, jnp.zeros((seq, rest), F32)], axis=1)
    return cos_t, sin_t


def kernel(x, ffn_norm, ffn_w_gate, ffn_w_up, ffn_w_down, mix_norm, w_in, sg_norm, sg_w, sg_b, q_norm, k_norm, gla_w_gate, gla_b_gate, gla_out_norm, w_out):
    batch, seq, d = x.shape
    depth = w_in.shape[0]
    n = batch * seq
    sg_width = sg_norm.shape[-1]
    gla_qk = gla_w_gate.shape[-1]
    gla_width = (gla_qk // GLA_DK) * HEAD_DIM
    att_width = d - sg_width - gla_width
    n_in = w_in.shape[-1]
    assert n_in == 2 * sg_width + 3 * att_width + 2 * gla_qk + 2 * gla_width + GLA_GATE_RANK
    widths = (sg_width, 3 * att_width, 2 * gla_qk + 2 * gla_width + LANES)

    ffn_w = (ffn_w_gate, ffn_w_up, ffn_w_down)
    ffn_b = tuple(w[0, 0].astype(BF16) for w in ffn_w)
    w_in_b = w_in.astype(BF16)
    w_out_b = w_out.astype(BF16)
    gla_wg = jnp.pad(gla_w_gate, ((0, 0), (0, LANES - GLA_GATE_RANK), (0, 0))).astype(BF16)
    ffn_g = ffn_norm.reshape(depth, 2, 1, d)
    mix_g = mix_norm.reshape(depth, 1, d)
    sg_g = sg_norm.reshape(depth, 1, sg_width)
    sg_bt = jnp.swapaxes(sg_b, 1, 2)
    q_g = q_norm.reshape(depth, 1, HEAD_DIM)
    k_g = k_norm.reshape(depth, 1, HEAD_DIM)
    gla_bg = gla_b_gate.reshape(depth, 1, gla_qk)
    gla_og = gla_out_norm.reshape(depth, 1, HEAD_DIM)
    cos_t, sin_t = _rope_tables(seq)

    h = x.reshape(n, d)
    for l in range(depth):
        h, ffn_b = _ffn(h, ffn_g, ffn_b, ffn_w, (l, 1), l, 0)
        ya, za, zg, zd = _inproj(h, mix_g, w_in_b, cos_t, sin_t, q_g, k_g, sg_g, sg_w, sg_bt, l, widths)
        yb = _attention(za, zd)
        yc = _gla(zg, gla_wg, gla_bg, gla_og, l, batch)
        h = _outproj(h, (ya, yb, yc), w_out_b, l)
        h, ffn_b = _ffn(h, ffn_g, ffn_b, ffn_w, (l + 1, 0) if l + 1 < depth else None, l, 1)
    return h.reshape(batch, seq, d)
```

```python
import functools

import jax
import jax.numpy as jnp
from jax import lax
from jax.experimental import pallas as pl
from jax.experimental.pallas import tpu as pltpu

F32 = jnp.float32
BF16 = jnp.bfloat16

EPS = 1e-6
HEAD_DIM = 128
LANES = 128
SG_CHUNK = 128
ATT_DILATIONS = (1, 4, 16)
ATT_STEPS = 128
ATT_RESIDUES = 16
ROPE_THETA = 500000.0
ROPE_DIMS = HEAD_DIM // 4
GLA_DK = 64
GLA_CHUNK = 64
GLA_GATE_RANK = 16
GLA_GATE_NORMALIZER = 16.0
MASK_VALUE = -1e30

VMEM_LIMIT_BYTES = 56 * 1024 * 1024

FFN_TM, FFN_TF = 1024, 512
FFN_NORM_CHUNKS = 4
PROJ_TM = 512
GLA_ROWS = 4096
ATT_GROUP = 8
LOG2_E = 1.4426950408889634


def _params(*semantics):
    return pltpu.CompilerParams(dimension_semantics=semantics, vmem_limit_bytes=VMEM_LIMIT_BYTES)


def _rms_norm(x, g):
    return x * lax.rsqrt(jnp.mean(x * x, axis=-1, keepdims=True) + EPS) * g


def _dot(a, b):
    return jnp.dot(a, b, preferred_element_type=F32)


def _dot_nt(a, b):
    return lax.dot_general(a, b, (((1,), (1,)), ((), ())), preferred_element_type=F32)


def _dot_tn(a, b):
    return lax.dot_general(a, b, (((0,), (0,)), ((), ())), preferred_element_type=F32)


def _ffn_kernel(x_ref, g_ref, wg_ref, wu_ref, wd_ref, *rest):
    n_next = (len(rest) - 2) // 2
    next_f32, o_ref, next_bf16, h_ref = rest[:n_next], rest[n_next], rest[n_next + 1:-1], rest[-1]

    def cast_next():
        for src, dst in zip(next_f32, next_bf16):
            dst[...] = src[...].astype(BF16)

    first = pl.program_id(1) == 0

    def half_swiglu(h):
        gate = _dot(h, wg_ref[...])
        up = _dot(h, wu_ref[...])
        return ((gate * jax.nn.sigmoid(gate)) * (0.5 * up)).astype(BF16)

    @pl.when(first)
    def _():
        cast_next()
        rows = x_ref.shape[0] // FFN_NORM_CHUNKS
        acts = []
        for c in range(FFN_NORM_CHUNKS):
            rs = slice(c * rows, (c + 1) * rows)
            h = _rms_norm(x_ref[rs, :], g_ref[...]).astype(BF16)
            h_ref[rs, :] = h
            acts.append(half_swiglu(h))
        for c, act in enumerate(acts):
            rs = slice(c * rows, (c + 1) * rows)
            o_ref[rs, :] = x_ref[rs, :] + _dot(act, wd_ref[...])

    @pl.when(jnp.logical_not(first))
    def _():
        cast_next()
        o_ref[...] += _dot(half_swiglu(h_ref[...]), wd_ref[...])


def _ffn(x, norm_g, weights, stacked_f32, nxt, l, j):
    n, d = x.shape
    w_gate, w_up, w_down = weights
    f = w_gate.shape[-1]
    tm, tf = min(FFN_TM, n), FFN_TF
    grid = (n // tm, f // tf)
    src, cast = (), []
    if nxt:
        src = tuple(stacked_f32)
        up_rows, down_rows = d // grid[0], f // (grid[0] * grid[1])
        assert up_rows * grid[0] == d and down_rows * grid[0] * grid[1] == f
        assert up_rows % 16 == 0 and down_rows % 16 == 0
        tile = ((up_rows, tf), lambda i, k: (i, k))
        cast = [tile, tile, ((down_rows, d), lambda i, k: (i * grid[1] + k, 0))]
    src_specs = [pl.BlockSpec((None, None) + b, lambda i, k, m=m: nxt + m(i, k)) for b, m in cast]
    dst_specs = [pl.BlockSpec(b, m) for b, m in cast]
    out = pl.pallas_call(
        _ffn_kernel,
        grid=grid,
        in_specs=[
            pl.BlockSpec((tm, d), lambda i, k: (i, 0)),
            pl.BlockSpec((None, None, 1, d), lambda i, k: (l, j, 0, 0)),
            pl.BlockSpec((d, tf), lambda i, k: (0, k)),
            pl.BlockSpec((d, tf), lambda i, k: (0, k)),
            pl.BlockSpec((tf, d), lambda i, k: (k, 0)),
        ] + src_specs,
        out_specs=[pl.BlockSpec((tm, d), lambda i, k: (i, 0))] + dst_specs,
        out_shape=[jax.ShapeDtypeStruct((n, d), F32)] + [jax.ShapeDtypeStruct(w.shape[2:], BF16) for w in src],
        scratch_shapes=[pltpu.VMEM((tm, d), BF16)],
        compiler_params=_params("parallel", "arbitrary"),
        name="ffn",
    )(x, norm_g, w_gate, w_up, w_down, *src)
    return out[0], tuple(out[1:])


def _inproj_kernel(x_ref, g_ref, w_ref, cos_ref, sin_ref, qg_ref, kg_ref, sgn_ref, sgw_ref, sgbt_ref,
                   ya_ref, za_ref, zg_ref, zd_ref, stage):
    h = _rms_norm(x_ref[...], g_ref[...]).astype(BF16)
    rows, sgw = ya_ref.shape
    c0 = 2 * sgw
    c1 = c0 + za_ref.shape[1]
    za = _dot(h, w_ref[:, c0:c1])
    zs = _dot(h, w_ref[:, :c0])
    c2 = c1 + zg_ref.shape[1] - LANES
    zg_ref[:, :c2 - c1] = _dot(h, w_ref[:, c1:c2]).astype(BF16)
    zg_ref[:, c2 - c1:] = jnp.zeros((rows, LANES), BF16)
    zg_ref[:, c2 - c1:c2 - c1 + GLA_GATE_RANK] = _dot(h, w_ref[:, c2:]).astype(BF16)

    heads = za_ref.shape[1] // (3 * HEAD_DIM)
    lane = lax.broadcasted_iota(jnp.int32, (1, HEAD_DIM), 1)
    half = ROPE_DIMS // 2
    cos = cos_ref[...]
    sin = sin_ref[...]
    res = zd_ref.shape[0]
    per = rows // res
    pitch = stage.shape[1] // res
    for j in range(3 * heads):
        cols = slice(j * HEAD_DIM, (j + 1) * HEAD_DIM)
        t = za[:, cols]
        if j < 2 * heads:
            gain, scale = (qg_ref, HEAD_DIM ** -0.5 * LOG2_E) if j < heads else (kg_ref, None)
            t = _rms_norm(t, gain[...])
            partner = jnp.where(lane < half, pltpu.roll(t, HEAD_DIM - half, 1), pltpu.roll(t, half, 1))
            t = t * cos + partner * sin
            if scale is not None:
                t = t * scale
        za_ref[:, cols] = t.astype(BF16)
        buf = stage.at[j % stage.shape[0]]
        for i in range(per):
            buf[pl.ds(i, res, stride=pitch), :] = t[i * res:(i + 1) * res, :]
        for r in range(res):
            zd_ref[r, :, cols] = buf[r * pitch:r * pitch + per, :].astype(BF16)

    t = SG_CHUNK
    chunks = rows // t
    causal = lax.broadcasted_iota(jnp.int32, (t, t), 1) <= lax.broadcasted_iota(jnp.int32, (t, t), 0)
    for g in range(sgw // HEAD_DIM):
        cu = slice(g * HEAD_DIM, (g + 1) * HEAD_DIM)
        cv = slice(sgw + g * HEAD_DIM, sgw + (g + 1) * HEAD_DIM)
        v = _rms_norm(jax.nn.gelu(zs[:, cv]), sgn_ref[:, cu]).astype(BF16)
        v = jnp.concatenate([v[c * t:(c + 1) * t] for c in range(chunks)], axis=1)
        sv = _dot(jnp.where(causal, sgw_ref[g], 0.0).astype(BF16), v) + sgbt_ref[:, g:g + 1]
        sv = jnp.concatenate([sv[:, c * HEAD_DIM:(c + 1) * HEAD_DIM] for c in range(chunks)], axis=0)
        ya_ref[:, cu] = (jax.nn.gelu(zs[:, cu]) * sv).astype(BF16)


def _inproj(x, norm_g, w_in, cos_t, sin_t, q_gain, k_gain, sg_norm, sg_w, sg_bt, l, widths):
    n, d = x.shape
    c = w_in.shape[-1]
    seq = cos_t.shape[0]
    batch = n // seq
    sgw = sg_norm.shape[-1]
    groups = sgw // HEAD_DIM
    res = ATT_RESIDUES
    tm = min(PROJ_TM, seq)
    tiles = seq // tm
    per = tm // res
    assert seq % tm == 0 and tm % SG_CHUNK == 0 and per % 16 == 0
    pitch = per + 8

    def layer(*block):
        return pl.BlockSpec((None,) + block, lambda i: (l,) + (0,) * len(block))

    return pl.pallas_call(
        _inproj_kernel,
        grid=(n // tm,),
        in_specs=[
            pl.BlockSpec((tm, d), lambda i: (i, 0)),
            layer(1, d),
            pl.BlockSpec((None, d, c), lambda i: (l, 0, 0), pipeline_mode=pl.Buffered(1)),
            pl.BlockSpec((tm, HEAD_DIM), lambda i: (i % tiles, 0)),
            pl.BlockSpec((tm, HEAD_DIM), lambda i: (i % tiles, 0)),
            layer(1, HEAD_DIM),
            layer(1, HEAD_DIM),
            layer(1, sgw),
            layer(groups, SG_CHUNK, SG_CHUNK),
            layer(SG_CHUNK, groups),
        ],
        out_specs=[pl.BlockSpec((tm, w), lambda i: (i, 0)) for w in widths]
        + [pl.BlockSpec((None, res, per, widths[1]), lambda i: (i // tiles, 0, i % tiles, 0))],
        out_shape=[jax.ShapeDtypeStruct((n, w), BF16) for w in widths]
        + [jax.ShapeDtypeStruct((batch, res, seq // res, widths[1]), BF16)],
        scratch_shapes=[pltpu.VMEM((2, res * pitch, HEAD_DIM), F32)],
        compiler_params=_params("parallel"),
        name="in_proj",
    )(x, norm_g, w_in, cos_t, sin_t, q_gain, k_gain, sg_norm, sg_w, sg_bt)


def _att_kernel(qn, kn, v_ref, qd, kd, vd, o_ref, acc, m_s, l_s, s_buf, mb_buf):
    seq = qn.shape[0]
    w = ATT_STEPS
    res = ATT_RESIDUES
    per = seq // res
    pitch = acc.shape[0] // res

    row = lax.broadcasted_iota(jnp.int32, (w, w), 0)
    col = lax.broadcasted_iota(jnp.int32, (w, w), 1)
    group = s_buf.shape[0]

    def run_branch(load_qk, load_v, load_state, store, order_diff):
        def scores(g):
            for u in range(group):
                q, k2, has_prev = load_qk(g * group + u)
                s = _dot_nt(q, k2)
                prev_min = jnp.where(has_prev, 0, 2 * w)
                s = jnp.concatenate([jnp.where(order_diff >= prev_min, s[:, :w], MASK_VALUE),
                                     jnp.where(order_diff <= 0, s[:, w:], MASK_VALUE)], axis=1)
                s_buf[u] = s
                mb_buf[u] = jnp.broadcast_to(jnp.max(s, axis=1, keepdims=True), (w, HEAD_DIM))

        def consume(g):
            for u in range(group):
                i = g * group + u
                old = load_state(i)
                m_blk = mb_buf[u]
                m_new = m_blk if old is None else jnp.maximum(old[0], m_blk)
                p = jnp.exp2(s_buf[u] - jnp.concatenate([m_new, m_new], axis=1))
                p_sum = jnp.sum(p, axis=1, keepdims=True)
                pv = _dot(p.astype(BF16), load_v(i))
                if old is None:
                    store(i, m_new, jnp.broadcast_to(p_sum, (w, HEAD_DIM)), pv)
                else:
                    alpha = jnp.exp2(old[0] - m_new)
                    store(i, m_new, alpha * old[1] + p_sum, alpha * old[2] + pv)

        def body(g, carry):
            consume(g - 1)
            scores(g)
            return carry

        groups = seq // (w * group)
        scores(0)
        lax.fori_loop(1, groups, body, 0)
        consume(groups - 1)

    for d in ATT_DILATIONS[:0:-1]:
        first = d == ATT_DILATIONS[-1]
        pieces = res // d
        plen = w // pieces
        nb = per // plen
        order = lambda x: pieces * (x % plen) + x // plen

        def gather(ref, r, at, d=d, pieces=pieces, plen=plen):
            return jnp.concatenate([ref[c * d + r, pl.ds(at, plen), :] for c in range(pieces)], axis=0)

        def starts(i, nb=nb, plen=plen):
            n = i % nb
            return (i // nb, n, pl.multiple_of(n * plen, plen),
                    pl.multiple_of(jnp.where(n > 0, n - 1, n) * plen, plen))

        def state_rows(r, c, start, d=d, plen=plen):
            return pl.ds(pl.multiple_of((c * d + r) * pitch + start, 8), plen)

        def load_qk(i, gather=gather, starts=starts):
            r, n, start, pstart = starts(i)
            return gather(qd, r, start), jnp.concatenate([gather(kd, r, pstart), gather(kd, r, start)], axis=0), n > 0

        def load_v(i, gather=gather, starts=starts):
            r, _, start, pstart = starts(i)
            return jnp.concatenate([gather(vd, r, pstart), gather(vd, r, start)], axis=0)

        def load_state(i, first=first, pieces=pieces, starts=starts, state_rows=state_rows):
            if first:
                return None
            r, _, start, _ = starts(i)
            return tuple(jnp.concatenate([ref[state_rows(r, c, start), :] for c in range(pieces)], axis=0)
                         for ref in (m_s, l_s, acc))

        def store(i, *new, pieces=pieces, plen=plen, starts=starts, state_rows=state_rows):
            r, _, start, _ = starts(i)
            for ref, val in zip((m_s, l_s, acc), new):
                for c in range(pieces):
                    ref[state_rows(r, c, start), :] = val[c * plen:(c + 1) * plen, :]

        run_branch(load_qk, load_v, load_state, store, order(col) - order(row))

    def nat_starts(n):
        return pl.multiple_of(n * w, w), pl.multiple_of(jnp.where(n > 0, n - 1, n) * w, w)

    def nat_qk(n):
        start, pstart = nat_starts(n)
        return (qn[pl.ds(start, w), :],
                jnp.concatenate([kn[pl.ds(pstart, w), :], kn[pl.ds(start, w), :]], axis=0), n > 0)

    def nat_v(n):
        start, pstart = nat_starts(n)
        return jnp.concatenate([v_ref[pl.ds(pstart, w), :], v_ref[pl.ds(start, w), :]], axis=0)

    def nat_state(n):
        return tuple(jnp.concatenate([ref[pl.ds(n * (w // res) + j, res, stride=pitch), :]
                                      for j in range(w // res)], axis=0) for ref in (m_s, l_s, acc))

    def emit(n, m_new, l_new, acc_new):
        o_ref[pl.ds(nat_starts(n)[0], w), :] = (acc_new / l_new).astype(BF16)

    run_branch(nat_qk, nat_v, nat_state, emit, col - row)


def _attention(za, zd):
    n, w3 = za.shape
    batch, res, per, _ = zd.shape
    seq = n // batch
    width = w3 // 3
    heads = width // HEAD_DIM
    assert res == ATT_RESIDUES and per * res == seq and seq % (ATT_STEPS * res) == 0
    pitch = per + 8
    za3 = za.reshape(batch, seq, w3)
    blk = (None, seq, HEAD_DIM)
    blk_d = (None, res, per, HEAD_DIM)
    out = pl.pallas_call(
        _att_kernel,
        grid=(batch, heads),
        in_specs=[
            pl.BlockSpec(blk, lambda b, h: (b, 0, h)),
            pl.BlockSpec(blk, lambda b, h: (b, 0, heads + h)),
            pl.BlockSpec(blk, lambda b, h: (b, 0, 2 * heads + h)),
            pl.BlockSpec(blk_d, lambda b, h: (b, 0, 0, h)),
            pl.BlockSpec(blk_d, lambda b, h: (b, 0, 0, heads + h)),
            pl.BlockSpec(blk_d, lambda b, h: (b, 0, 0, 2 * heads + h)),
        ],
        out_specs=pl.BlockSpec(blk, lambda b, h: (b, 0, h)),
        out_shape=jax.ShapeDtypeStruct((batch, seq, width), BF16),
        scratch_shapes=[pltpu.VMEM((res * pitch, HEAD_DIM), F32) for _ in range(3)]
        + [pltpu.VMEM((ATT_GROUP, ATT_STEPS, 2 * ATT_STEPS), F32), pltpu.VMEM((ATT_GROUP, ATT_STEPS, HEAD_DIM), F32)],
        compiler_params=_params("parallel", "parallel"),
        name="dilated_attention",
    )(za3, za3, za3, zd, zd, zd)
    return out.reshape(n, width)


def _gla_kernel(q_ref, k_ref, v_ref, g_ref, r_ref, wg_ref, bg_ref, og_ref, o_ref, st_ref, o_acc):
    rows = q_ref.shape[0]
    ck = GLA_CHUNK
    dv = HEAD_DIM

    @pl.when(pl.program_id(2) == 0)
    def _():
        st_ref[...] = jnp.zeros_like(st_ref)

    x = _dot(r_ref[...], wg_ref[...]) + bg_ref[...]
    log_a = (jnp.minimum(x, 0.0) - jnp.log(1.0 + jnp.exp(-jnp.abs(x)))) / GLA_GATE_NORMALIZER
    hi = log_a.astype(BF16)
    lo = (log_a - hi.astype(F32)).astype(BF16)
    grp = 4 * ck
    r_i = lax.broadcasted_iota(jnp.int32, (grp, grp), 0)
    c_i = lax.broadcasted_iota(jnp.int32, (grp, grp), 1)
    tri = jnp.where(((r_i // ck) == (c_i // ck)) & (c_i <= r_i), 1.0, 0.0).astype(BF16)
    b = jnp.concatenate([_dot(tri, hi[s * grp:(s + 1) * grp]) + _dot(tri, lo[s * grp:(s + 1) * grp])
                         for s in range(rows // grp)], axis=0)

    q = q_ref[...].astype(F32) * GLA_DK ** -0.5
    k = k_ref[...].astype(F32)
    q_dec = (q * jnp.exp(b)).astype(BF16)
    k_inv = (k * jnp.exp(-b)).astype(BF16)

    lane = lax.broadcasted_iota(jnp.int32, (1, 2 * GLA_DK), 1)
    head0 = lane < GLA_DK
    ar = lax.broadcasted_iota(jnp.int32, (ck, 2 * ck), 0)
    ac = lax.broadcasted_iota(jnp.int32, (ck, 2 * ck), 1)
    causal = (ac % ck) <= ar
    sr = lax.broadcasted_iota(jnp.int32, (2 * dv, 2 * GLA_DK), 0)
    sc = lax.broadcasted_iota(jnp.int32, (2 * dv, 2 * GLA_DK), 1)
    own = (sr < dv) == (sc < GLA_DK)
    zeros_v = jnp.zeros((ck, dv), BF16)

    chunks = [slice(c * ck, (c + 1) * ck) for c in range(rows // ck)]
    scores, incs, decays = [], [], []
    for rs in chunks:
        ki = k_inv[rs]
        ki2 = jnp.concatenate([jnp.where(head0, ki, 0), jnp.where(head0, 0, ki)], axis=0)
        scores.append(_dot_nt(q_dec[rs], ki2))
    for rs in chunks:
        b_last = b[rs.stop - 1:rs.stop, :]
        k_dec = (k[rs] * jnp.exp(b_last - b[rs])).astype(BF16)
        incs.append(_dot_tn(v_ref[rs, :], k_dec))
        decays.append(jnp.exp(b_last))
    for rs, a in zip(chunks, scores):
        vp = v_ref[rs, :]
        v_bd = jnp.concatenate(
            [jnp.concatenate([vp[:, :dv], zeros_v], axis=1),
             jnp.concatenate([zeros_v, vp[:, dv:]], axis=1)], axis=0)
        o_acc[rs, :] = _dot(jnp.where(causal, a, 0.0).astype(BF16), v_bd)
    state = st_ref[...]
    for rs, inc, decay in zip(chunks, incs, decays):
        o_acc[rs, :] += _dot_nt(q_dec[rs], state.astype(BF16))
        state = state * decay + jnp.where(own, inc, 0.0)
    st_ref[...] = state

    for h in range(2):
        cs = slice(h * dv, (h + 1) * dv)
        g = g_ref[:, cs].astype(F32)
        o = _rms_norm(o_acc[:, cs], og_ref[...]) * (g * jax.nn.sigmoid(g))
        o_ref[:, cs] = o.astype(BF16)


def _gla(zg, w_gate, b_gate, out_gain, l, batch):
    n, _ = zg.shape
    seq = n // batch
    dkp = 2 * GLA_DK
    dvp = 2 * HEAD_DIM
    pairs = w_gate.shape[-1] // dkp
    width = pairs * dvp
    rows = min(GLA_ROWS, seq)
    zg3 = zg.reshape(batch, seq, zg.shape[1])
    qk_blocks = 2 * pairs * dkp // dvp
    out = pl.pallas_call(
        _gla_kernel,
        grid=(batch, pairs, seq // rows),
        in_specs=[
            pl.BlockSpec((None, rows, dkp), lambda b, p, t: (b, t, p)),
            pl.BlockSpec((None, rows, dkp), lambda b, p, t: (b, t, pairs + p)),
            pl.BlockSpec((None, rows, dvp), lambda b, p, t: (b, t, qk_blocks + p)),
            pl.BlockSpec((None, rows, dvp), lambda b, p, t: (b, t, qk_blocks + pairs + p)),
            pl.BlockSpec((None, rows, LANES), lambda b, p, t: (b, t, (2 * pairs * dkp + 2 * width) // LANES)),
            pl.BlockSpec((None, LANES, dkp), lambda b, p, t: (l, 0, p)),
            pl.BlockSpec((None, 1, dkp), lambda b, p, t: (l, 0, p)),
            pl.BlockSpec((None, 1, HEAD_DIM), lambda b, p, t: (l, 0, 0)),
        ],
        out_specs=pl.BlockSpec((None, rows, dvp), lambda b, p, t: (b, t, p)),
        out_shape=jax.ShapeDtypeStruct((batch, seq, width), BF16),
        scratch_shapes=[pltpu.VMEM((dvp, dkp), F32), pltpu.VMEM((rows, dvp), F32)],
        compiler_params=_params("parallel", "parallel", "arbitrary"),
        name="gated_linear_attention",
    )(zg3, zg3, zg3, zg3, zg3, w_gate, b_gate, out_gain)
    return out.reshape(n, width)


def _outproj_kernel(x_ref, ya_ref, yb_ref, yc_ref, w_ref, o_ref):
    acc = x_ref[...]
    c = 0
    for y_ref in (ya_ref, yb_ref, yc_ref):
        wdt = y_ref.shape[1]
        acc = acc + _dot(y_ref[...], w_ref[c:c + wdt, :])
        c += wdt
    o_ref[...] = acc


def _outproj(x, ys, w_out, l):
    n, d = x.shape
    tm = min(PROJ_TM, n)
    return pl.pallas_call(
        _outproj_kernel,
        grid=(n // tm,),
        in_specs=[pl.BlockSpec((tm, d), lambda i: (i, 0))]
        + [pl.BlockSpec((tm, y.shape[1]), lambda i: (i, 0)) for y in ys]
        + [pl.BlockSpec((None, w_out.shape[1], d), lambda i: (l, 0, 0), pipeline_mode=pl.Buffered(1))],
        out_specs=pl.BlockSpec((tm, d), lambda i: (i, 0)),
        out_shape=jax.ShapeDtypeStruct((n, d), F32),
        compiler_params=_params("parallel"),
        name="out_proj",
    )(x, *ys, w_out)


def _rope_tables(seq):
    pos = jnp.arange(seq, dtype=F32)
    inv_freq = ROPE_THETA ** (-jnp.arange(0, ROPE_DIMS, 2, dtype=F32) / ROPE_DIMS)
    ang = pos[:, None] * inv_freq[None, :]
    cos, sin = jnp.cos(ang), jnp.sin(ang)
    rest = HEAD_DIM - ROPE_DIMS
    cos_t = jnp.concatenate([cos, cos, jnp.ones((seq, rest), F32)], axis=1)
    sin_t = jnp.concatenate([-sin, sin, jnp.zeros((seq, rest), F32)], axis=1)
    return cos_t, sin_t


def kernel(x, ffn_norm, ffn_w_gate, ffn_w_up, ffn_w_down, mix_norm, w_in, sg_norm, sg_w, sg_b, q_norm, k_norm, gla_w_gate, gla_b_gate, gla_out_norm, w_out):
    batch, seq, d = x.shape
    depth = w_in.shape[0]
    n = batch * seq
    sg_width = sg_norm.shape[-1]
    gla_qk = gla_w_gate.shape[-1]
    gla_width = (gla_qk // GLA_DK) * HEAD_DIM
    att_width = d - sg_width - gla_width
    n_in = w_in.shape[-1]
    assert n_in == 2 * sg_width + 3 * att_width + 2 * gla_qk + 2 * gla_width + GLA_GATE_RANK
    widths = (sg_width, 3 * att_width, 2 * gla_qk + 2 * gla_width + LANES)

    ffn_w = (ffn_w_gate, ffn_w_up, ffn_w_down)
    ffn_b = tuple(w[0, 0].astype(BF16) for w in ffn_w)
    w_in_b = w_in.astype(BF16)
    w_out_b = w_out.astype(BF16)
    gla_wg = jnp.pad(gla_w_gate, ((0, 0), (0, LANES - GLA_GATE_RANK), (0, 0))).astype(BF16)
    ffn_g = ffn_norm.reshape(depth, 2, 1, d)
    mix_g = mix_norm.reshape(depth, 1, d)
    sg_g = sg_norm.reshape(depth, 1, sg_width)
    sg_bt = jnp.swapaxes(sg_b, 1, 2)
    q_g = q_norm.reshape(depth, 1, HEAD_DIM)
    k_g = k_norm.reshape(depth, 1, HEAD_DIM)
    gla_bg = gla_b_gate.reshape(depth, 1, gla_qk)
    gla_og = gla_out_norm.reshape(depth, 1, HEAD_DIM)
    cos_t, sin_t = _rope_tables(seq)

    h = x.reshape(n, d)
    for l in range(depth):
        h, ffn_b = _ffn(h, ffn_g, ffn_b, ffn_w, (l, 1), l, 0)
        ya, za, zg, zd = _inproj(h, mix_g, w_in_b, cos_t, sin_t, q_g, k_g, sg_g, sg_w, sg_bt, l, widths)
        yb = _attention(za, zd)
        yc = _gla(zg, gla_wg, gla_bg, gla_og, l, batch)
        h = _outproj(h, (ya, yb, yc), w_out_b, l)
        h, ffn_b = _ffn(h, ffn_g, ffn_b, ffn_w, (l + 1, 0) if l + 1 < depth else None, l, 1)
    return h.reshape(batch, seq, d)
```

```python
import functools

import jax
import jax.numpy as jnp
from jax import lax
from jax.experimental import pallas as pl
from jax.experimental.pallas import tpu as pltpu

F32 = jnp.float32
BF16 = jnp.bfloat16

EPS = 1e-6
HEAD_DIM = 128
LANES = 128
SG_CHUNK = 128
ATT_DILATIONS = (1, 4, 16)
ATT_STEPS = 128
ATT_RESIDUES = 16
ROPE_THETA = 500000.0
ROPE_DIMS = HEAD_DIM // 4
GLA_DK = 64
GLA_CHUNK = 64
GLA_GATE_RANK = 16
GLA_GATE_NORMALIZER = 16.0
MASK_VALUE = -1e30

VMEM_LIMIT_BYTES = 56 * 1024 * 1024

FFN_TM, FFN_TF = 1024, 512
FFN_NORM_CHUNKS = 4
PROJ_TM = 512
GLA_ROWS = 4096
ATT_GROUP = 8
LOG2_E = 1.4426950408889634


def _params(*semantics):
    return pltpu.CompilerParams(dimension_semantics=semantics, vmem_limit_bytes=VMEM_LIMIT_BYTES)


def _rms_norm(x, g):
    return x * lax.rsqrt(jnp.mean(x * x, axis=-1, keepdims=True) + EPS) * g


def _dot(a, b):
    return jnp.dot(a, b, preferred_element_type=F32)


def _dot_nt(a, b):
    return lax.dot_general(a, b, (((1,), (1,)), ((), ())), preferred_element_type=F32)


def _dot_tn(a, b):
    return lax.dot_general(a, b, (((0,), (0,)), ((), ())), preferred_element_type=F32)


def _ffn_kernel(x_ref, g_ref, wg_ref, wu_ref, wd_ref, *rest):
    n_next = (len(rest) - 2) // 2
    next_f32, o_ref, next_bf16, h_ref = rest[:n_next], rest[n_next], rest[n_next + 1:-1], rest[-1]

    def cast_next():
        for src, dst in zip(next_f32, next_bf16):
            dst[...] = src[...].astype(BF16)

    first = pl.program_id(1) == 0

    def half_swiglu(h):
        gate = _dot(h, wg_ref[...])
        up = _dot(h, wu_ref[...])
        return ((gate * jax.nn.sigmoid(gate)) * (0.5 * up)).astype(BF16)

    @pl.when(first)
    def _():
        cast_next()
        rows = x_ref.shape[0] // FFN_NORM_CHUNKS
        acts = []
        for c in range(FFN_NORM_CHUNKS):
            rs = slice(c * rows, (c + 1) * rows)
            h = _rms_norm(x_ref[rs, :], g_ref[...]).astype(BF16)
            h_ref[rs, :] = h
            acts.append(half_swiglu(h))
        for c, act in enumerate(acts):
            rs = slice(c * rows, (c + 1) * rows)
            o_ref[rs, :] = x_ref[rs, :] + _dot(act, wd_ref[...])

    @pl.when(jnp.logical_not(first))
    def _():
        cast_next()
        o_ref[...] += _dot(half_swiglu(h_ref[...]), wd_ref[...])


def _ffn(x, norm_g, weights, stacked_f32, nxt, l, j):
    n, d = x.shape
    w_gate, w_up, w_down = weights
    f = w_gate.shape[-1]
    tm, tf = min(FFN_TM, n), FFN_TF
    grid = (n // tm, f // tf)
    src, cast = (), []
    if nxt:
        src = tuple(stacked_f32)
        up_rows, down_rows = d // grid[0], f // (grid[0] * grid[1])
        assert up_rows * grid[0] == d and down_rows * grid[0] * grid[1] == f
        assert up_rows % 16 == 0 and down_rows % 16 == 0
        tile = ((up_rows, tf), lambda i, k: (i, k))
        cast = [tile, tile, ((down_rows, d), lambda i, k: (i * grid[1] + k, 0))]
    src_specs = [pl.BlockSpec((None, None) + b, lambda i, k, m=m: nxt + m(i, k)) for b, m in cast]
    dst_specs = [pl.BlockSpec(b, m) for b, m in cast]
    out = pl.pallas_call(
        _ffn_kernel,
        grid=grid,
        in_specs=[
            pl.BlockSpec((tm, d), lambda i, k: (i, 0)),
            pl.BlockSpec((None, None, 1, d), lambda i, k: (l, j, 0, 0)),
            pl.BlockSpec((d, tf), lambda i, k: (0, k)),
            pl.BlockSpec((d, tf), lambda i, k: (0, k)),
            pl.BlockSpec((tf, d), lambda i, k: (k, 0)),
        ] + src_specs,
        out_specs=[pl.BlockSpec((tm, d), lambda i, k: (i, 0))] + dst_specs,
        out_shape=[jax.ShapeDtypeStruct((n, d), F32)] + [jax.ShapeDtypeStruct(w.shape[2:], BF16) for w in src],
        scratch_shapes=[pltpu.VMEM((tm, d), BF16)],
        compiler_params=_params("parallel", "arbitrary"),
        name="ffn",
    )(x, norm_g, w_gate, w_up, w_down, *src)
    return out[0], tuple(out[1:])


def _inproj_kernel(x_ref, g_ref, w_ref, cos_ref, sin_ref, qg_ref, kg_ref, sgn_ref, sgw_ref, sgbt_ref,
                   ya_ref, za_ref, zg_ref, zd_ref, stage):
    h = _rms_norm(x_ref[...], g_ref[...]).astype(BF16)
    rows, sgw = ya_ref.shape
    c0 = 2 * sgw
    c1 = c0 + za_ref.shape[0] * HEAD_DIM
    za = _dot(h, w_ref[:, c0:c1])
    zs = _dot(h, w_ref[:, :c0])
    c2 = c1 + zg_ref.shape[1] - LANES
    zg_ref[:, :c2 - c1] = _dot(h, w_ref[:, c1:c2]).astype(BF16)
    zg_ref[:, c2 - c1:] = jnp.zeros((rows, LANES), BF16)
    zg_ref[:, c2 - c1:c2 - c1 + GLA_GATE_RANK] = _dot(h, w_ref[:, c2:]).astype(BF16)

    heads = za_ref.shape[0] // 3
    lane = lax.broadcasted_iota(jnp.int32, (1, HEAD_DIM), 1)
    half = ROPE_DIMS // 2
    cos = cos_ref[...]
    sin = sin_ref[...]
    res = zd_ref.shape[1]
    per = rows // res
    pitch = stage.shape[1] // res
    for j in range(3 * heads):
        cols = slice(j * HEAD_DIM, (j + 1) * HEAD_DIM)
        t = za[:, cols]
        if j < 2 * heads:
            gain, scale = (qg_ref, HEAD_DIM ** -0.5 * LOG2_E) if j < heads else (kg_ref, None)
            t = _rms_norm(t, gain[...])
            partner = jnp.where(lane < half, pltpu.roll(t, HEAD_DIM - half, 1), pltpu.roll(t, half, 1))
            t = t * cos + partner * sin
            if scale is not None:
                t = t * scale
        za_ref[j] = t.astype(BF16)
        buf = stage.at[j % stage.shape[0]]
        for i in range(per):
            buf[pl.ds(i, res, stride=pitch), :] = t[i * res:(i + 1) * res, :]
        for r in range(res):
            zd_ref[j, r] = buf[r * pitch:r * pitch + per, :].astype(BF16)

    t = SG_CHUNK
    chunks = rows // t
    causal = lax.broadcasted_iota(jnp.int32, (t, t), 1) <= lax.broadcasted_iota(jnp.int32, (t, t), 0)
    for g in range(sgw // HEAD_DIM):
        cu = slice(g * HEAD_DIM, (g + 1) * HEAD_DIM)
        cv = slice(sgw + g * HEAD_DIM, sgw + (g + 1) * HEAD_DIM)
        v = _rms_norm(jax.nn.gelu(zs[:, cv]), sgn_ref[:, cu]).astype(BF16)
        v = jnp.concatenate([v[c * t:(c + 1) * t] for c in range(chunks)], axis=1)
        sv = _dot(jnp.where(causal, sgw_ref[g], 0.0).astype(BF16), v) + sgbt_ref[:, g:g + 1]
        sv = jnp.concatenate([sv[:, c * HEAD_DIM:(c + 1) * HEAD_DIM] for c in range(chunks)], axis=0)
        ya_ref[:, cu] = (jax.nn.gelu(zs[:, cu]) * sv).astype(BF16)


def _inproj(x, norm_g, w_in, cos_t, sin_t, q_gain, k_gain, sg_norm, sg_w, sg_bt, l, widths):
    n, d = x.shape
    c = w_in.shape[-1]
    seq = cos_t.shape[0]
    batch = n // seq
    sgw = sg_norm.shape[-1]
    groups = sgw // HEAD_DIM
    res = ATT_RESIDUES
    tm = min(PROJ_TM, seq)
    tiles = seq // tm
    per = tm // res
    assert seq % tm == 0 and tm % SG_CHUNK == 0 and per % 16 == 0
    pitch = per + 8
    slabs = widths[1] // HEAD_DIM

    def layer(*block):
        return pl.BlockSpec((None,) + block, lambda i: (l,) + (0,) * len(block))

    return pl.pallas_call(
        _inproj_kernel,
        grid=(n // tm,),
        in_specs=[
            pl.BlockSpec((tm, d), lambda i: (i, 0)),
            layer(1, d),
            pl.BlockSpec((None, d, c), lambda i: (l, 0, 0), pipeline_mode=pl.Buffered(1)),
            pl.BlockSpec((tm, HEAD_DIM), lambda i: (i % tiles, 0)),
            pl.BlockSpec((tm, HEAD_DIM), lambda i: (i % tiles, 0)),
            layer(1, HEAD_DIM),
            layer(1, HEAD_DIM),
            layer(1, sgw),
            layer(groups, SG_CHUNK, SG_CHUNK),
            layer(SG_CHUNK, groups),
        ],
        out_specs=[pl.BlockSpec((tm, widths[0]), lambda i: (i, 0)),
                   pl.BlockSpec((slabs, tm, HEAD_DIM), lambda i: (0, i, 0)),
                   pl.BlockSpec((tm, widths[2]), lambda i: (i, 0)),
                   pl.BlockSpec((slabs, None, res, per, HEAD_DIM), lambda i: (0, i // tiles, 0, i % tiles, 0))],
        out_shape=[jax.ShapeDtypeStruct((n, widths[0]), BF16),
                   jax.ShapeDtypeStruct((slabs, n, HEAD_DIM), BF16),
                   jax.ShapeDtypeStruct((n, widths[2]), BF16),
                   jax.ShapeDtypeStruct((slabs, batch, res, seq // res, HEAD_DIM), BF16)],
        scratch_shapes=[pltpu.VMEM((2, res * pitch, HEAD_DIM), F32)],
        compiler_params=_params("parallel"),
        name="in_proj",
    )(x, norm_g, w_in, cos_t, sin_t, q_gain, k_gain, sg_norm, sg_w, sg_bt)


def _att_kernel(qn, kn, v_ref, qd, kd, vd, o_ref, acc, m_s, l_s, s_buf, mb_buf):
    seq = qn.shape[0]
    w = ATT_STEPS
    res = ATT_RESIDUES
    per = seq // res
    pitch = acc.shape[0] // res

    row = lax.broadcasted_iota(jnp.int32, (w, w), 0)
    col = lax.broadcasted_iota(jnp.int32, (w, w), 1)
    group = s_buf.shape[0]

    def run_branch(load_qk, load_v, load_state, store, order_diff):
        def scores(g):
            for u in range(group):
                q, k2, has_prev = load_qk(g * group + u)
                s = _dot_nt(q, k2)
                prev_min = jnp.where(has_prev, 0, 2 * w)
                s = jnp.concatenate([jnp.where(order_diff >= prev_min, s[:, :w], MASK_VALUE),
                                     jnp.where(order_diff <= 0, s[:, w:], MASK_VALUE)], axis=1)
                s_buf[u] = s
                mb_buf[u] = jnp.broadcast_to(jnp.max(s, axis=1, keepdims=True), (w, HEAD_DIM))

        def consume(g):
            for u in range(group):
                i = g * group + u
                old = load_state(i)
                m_blk = mb_buf[u]
                m_new = m_blk if old is None else jnp.maximum(old[0], m_blk)
                p = jnp.exp2(s_buf[u] - jnp.concatenate([m_new, m_new], axis=1))
                p_sum = jnp.sum(p, axis=1, keepdims=True)
                pv = _dot(p.astype(BF16), load_v(i))
                if old is None:
                    store(i, m_new, jnp.broadcast_to(p_sum, (w, HEAD_DIM)), pv)
                else:
                    alpha = jnp.exp2(old[0] - m_new)
                    store(i, m_new, alpha * old[1] + p_sum, alpha * old[2] + pv)

        def body(g, carry):
            consume(g - 1)
            scores(g)
            return carry

        groups = seq // (w * group)
        scores(0)
        lax.fori_loop(1, groups, body, 0)
        consume(groups - 1)

    for d in ATT_DILATIONS[:0:-1]:
        first = d == ATT_DILATIONS[-1]
        pieces = res // d
        plen = w // pieces
        nb = per // plen
        order = lambda x: pieces * (x % plen) + x // plen

        def gather(ref, r, at, d=d, pieces=pieces, plen=plen):
            return jnp.concatenate([ref[c * d + r, pl.ds(at, plen), :] for c in range(pieces)], axis=0)

        def starts(i, nb=nb, plen=plen):
            n = i % nb
            return (i // nb, n, pl.multiple_of(n * plen, plen),
                    pl.multiple_of(jnp.where(n > 0, n - 1, n) * plen, plen))

        def state_rows(r, c, start, d=d, plen=plen):
            return pl.ds(pl.multiple_of((c * d + r) * pitch + start, 8), plen)

        def load_qk(i, gather=gather, starts=starts):
            r, n, start, pstart = starts(i)
            return gather(qd, r, start), jnp.concatenate([gather(kd, r, pstart), gather(kd, r, start)], axis=0), n > 0

        def load_v(i, gather=gather, starts=starts):
            r, _, start, pstart = starts(i)
            return jnp.concatenate([gather(vd, r, pstart), gather(vd, r, start)], axis=0)

        def load_state(i, first=first, pieces=pieces, starts=starts, state_rows=state_rows):
            if first:
                return None
            r, _, start, _ = starts(i)
            return tuple(jnp.concatenate([ref[state_rows(r, c, start), :] for c in range(pieces)], axis=0)
                         for ref in (m_s, l_s, acc))

        def store(i, *new, pieces=pieces, plen=plen, starts=starts, state_rows=state_rows):
            r, _, start, _ = starts(i)
            for ref, val in zip((m_s, l_s, acc), new):
                for c in range(pieces):
                    ref[state_rows(r, c, start), :] = val[c * plen:(c + 1) * plen, :]

        run_branch(load_qk, load_v, load_state, store, order(col) - order(row))

    def nat_starts(n):
        return pl.multiple_of(n * w, w), pl.multiple_of(jnp.where(n > 0, n - 1, n) * w, w)

    def nat_qk(n):
        start, pstart = nat_starts(n)
        return (qn[pl.ds(start, w), :],
                jnp.concatenate([kn[pl.ds(pstart, w), :], kn[pl.ds(start, w), :]], axis=0), n > 0)

    def nat_v(n):
        start, pstart = nat_starts(n)
        return jnp.concatenate([v_ref[pl.ds(pstart, w), :], v_ref[pl.ds(start, w), :]], axis=0)

    def nat_state(n):
        return tuple(jnp.concatenate([ref[pl.ds(n * (w // res) + j, res, stride=pitch), :]
                                      for j in range(w // res)], axis=0) for ref in (m_s, l_s, acc))

    def emit(n, m_new, l_new, acc_new):
        o_ref[pl.ds(nat_starts(n)[0], w), :] = (acc_new / l_new).astype(BF16)

    run_branch(nat_qk, nat_v, nat_state, emit, col - row)


def _attention(za, zd):
    slabs, n, _ = za.shape
    _, batch, res, per, _ = zd.shape
    seq = n // batch
    heads = slabs // 3
    assert res == ATT_RESIDUES and per * res == seq and seq % (ATT_STEPS * res) == 0
    pitch = per + 8
    za3 = za.reshape(slabs, batch, seq, HEAD_DIM)
    blk = (None, None, seq, HEAD_DIM)
    blk_d = (None, None, res, per, HEAD_DIM)
    out = pl.pallas_call(
        _att_kernel,
        grid=(batch, heads),
        in_specs=[
            pl.BlockSpec(blk, lambda b, h: (h, b, 0, 0)),
            pl.BlockSpec(blk, lambda b, h: (heads + h, b, 0, 0)),
            pl.BlockSpec(blk, lambda b, h: (2 * heads + h, b, 0, 0)),
            pl.BlockSpec(blk_d, lambda b, h: (h, b, 0, 0, 0)),
            pl.BlockSpec(blk_d, lambda b, h: (heads + h, b, 0, 0, 0)),
            pl.BlockSpec(blk_d, lambda b, h: (2 * heads + h, b, 0, 0, 0)),
        ],
        out_specs=pl.BlockSpec(blk, lambda b, h: (h, b, 0, 0)),
        out_shape=jax.ShapeDtypeStruct((heads, batch, seq, HEAD_DIM), BF16),
        scratch_shapes=[pltpu.VMEM((res * pitch, HEAD_DIM), F32) for _ in range(3)]
        + [pltpu.VMEM((ATT_GROUP, ATT_STEPS, 2 * ATT_STEPS), F32), pltpu.VMEM((ATT_GROUP, ATT_STEPS, HEAD_DIM), F32)],
        compiler_params=_params("parallel", "parallel"),
        name="dilated_attention",
    )(za3, za3, za3, zd, zd, zd)
    return out.reshape(heads, n, HEAD_DIM)


def _gla_kernel(q_ref, k_ref, v_ref, g_ref, r_ref, wg_ref, bg_ref, og_ref, o_ref, st_ref, o_acc):
    rows = q_ref.shape[0]
    ck = GLA_CHUNK
    dv = HEAD_DIM

    @pl.when(pl.program_id(2) == 0)
    def _():
        st_ref[...] = jnp.zeros_like(st_ref)

    x = _dot(r_ref[...], wg_ref[...]) + bg_ref[...]
    log_a = (jnp.minimum(x, 0.0) - jnp.log(1.0 + jnp.exp(-jnp.abs(x)))) / GLA_GATE_NORMALIZER
    hi = log_a.astype(BF16)
    lo = (log_a - hi.astype(F32)).astype(BF16)
    grp = 4 * ck
    r_i = lax.broadcasted_iota(jnp.int32, (grp, grp), 0)
    c_i = lax.broadcasted_iota(jnp.int32, (grp, grp), 1)
    tri = jnp.where(((r_i // ck) == (c_i // ck)) & (c_i <= r_i), 1.0, 0.0).astype(BF16)
    b = jnp.concatenate([_dot(tri, hi[s * grp:(s + 1) * grp]) + _dot(tri, lo[s * grp:(s + 1) * grp])
                         for s in range(rows // grp)], axis=0)

    q = q_ref[...].astype(F32) * GLA_DK ** -0.5
    k = k_ref[...].astype(F32)
    q_dec = (q * jnp.exp(b)).astype(BF16)
    k_inv = (k * jnp.exp(-b)).astype(BF16)

    lane = lax.broadcasted_iota(jnp.int32, (1, 2 * GLA_DK), 1)
    head0 = lane < GLA_DK
    ar = lax.broadcasted_iota(jnp.int32, (ck, 2 * ck), 0)
    ac = lax.broadcasted_iota(jnp.int32, (ck, 2 * ck), 1)
    causal = (ac % ck) <= ar
    sr = lax.broadcasted_iota(jnp.int32, (2 * dv, 2 * GLA_DK), 0)
    sc = lax.broadcasted_iota(jnp.int32, (2 * dv, 2 * GLA_DK), 1)
    own = (sr < dv) == (sc < GLA_DK)
    zeros_v = jnp.zeros((ck, dv), BF16)

    chunks = [slice(c * ck, (c + 1) * ck) for c in range(rows // ck)]
    scores, incs, decays = [], [], []
    for rs in chunks:
        ki = k_inv[rs]
        ki2 = jnp.concatenate([jnp.where(head0, ki, 0), jnp.where(head0, 0, ki)], axis=0)
        scores.append(_dot_nt(q_dec[rs], ki2))
    for rs in chunks:
        b_last = b[rs.stop - 1:rs.stop, :]
        k_dec = (k[rs] * jnp.exp(b_last - b[rs])).astype(BF16)
        incs.append(_dot_tn(v_ref[rs, :], k_dec))
        decays.append(jnp.exp(b_last))
    for rs, a in zip(chunks, scores):
        vp = v_ref[rs, :]
        v_bd = jnp.concatenate(
            [jnp.concatenate([vp[:, :dv], zeros_v], axis=1),
             jnp.concatenate([zeros_v, vp[:, dv:]], axis=1)], axis=0)
        o_acc[rs, :] = _dot(jnp.where(causal, a, 0.0).astype(BF16), v_bd)
    state = st_ref[...]
    for rs, inc, decay in zip(chunks, incs, decays):
        o_acc[rs, :] += _dot_nt(q_dec[rs], state.astype(BF16))
        state = state * decay + jnp.where(own, inc, 0.0)
    st_ref[...] = state

    for h in range(2):
        cs = slice(h * dv, (h + 1) * dv)
        g = g_ref[:, cs].astype(F32)
        o = _rms_norm(o_acc[:, cs], og_ref[...]) * (g * jax.nn.sigmoid(g))
        o_ref[:, cs] = o.astype(BF16)


def _gla(zg, w_gate, b_gate, out_gain, l, batch):
    n, _ = zg.shape
    seq = n // batch
    dkp = 2 * GLA_DK
    dvp = 2 * HEAD_DIM
    pairs = w_gate.shape[-1] // dkp
    width = pairs * dvp
    rows = min(GLA_ROWS, seq)
    zg3 = zg.reshape(batch, seq, zg.shape[1])
    qk_blocks = 2 * pairs * dkp // dvp
    out = pl.pallas_call(
        _gla_kernel,
        grid=(batch, pairs, seq // rows),
        in_specs=[
            pl.BlockSpec((None, rows, dkp), lambda b, p, t: (b, t, p)),
            pl.BlockSpec((None, rows, dkp), lambda b, p, t: (b, t, pairs + p)),
            pl.BlockSpec((None, rows, dvp), lambda b, p, t: (b, t, qk_blocks + p)),
            pl.BlockSpec((None, rows, dvp), lambda b, p, t: (b, t, qk_blocks + pairs + p)),
            pl.BlockSpec((None, rows, LANES), lambda b, p, t: (b, t, (2 * pairs * dkp + 2 * width) // LANES)),
            pl.BlockSpec((None, LANES, dkp), lambda b, p, t: (l, 0, p)),
            pl.BlockSpec((None, 1, dkp), lambda b, p, t: (l, 0, p)),
            pl.BlockSpec((None, 1, HEAD_DIM), lambda b, p, t: (l, 0, 0)),
        ],
        out_specs=pl.BlockSpec((None, rows, dvp), lambda b, p, t: (b, t, p)),
        out_shape=jax.ShapeDtypeStruct((batch, seq, width), BF16),
        scratch_shapes=[pltpu.VMEM((dvp, dkp), F32), pltpu.VMEM((rows, dvp), F32)],
        compiler_params=_params("parallel", "parallel", "arbitrary"),
        name="gated_linear_attention",
    )(zg3, zg3, zg3, zg3, zg3, w_gate, b_gate, out_gain)
    return out.reshape(n, width)


def _outproj_kernel(x_ref, ya_ref, yb_ref, yc_ref, w_ref, o_ref):
    acc = x_ref[...]
    c = 0
    for y_ref in (ya_ref, yb_ref, yc_ref):
        if len(y_ref.shape) == 3:
            y = jnp.concatenate([y_ref[j] for j in range(y_ref.shape[0])], axis=1)
        else:
            y = y_ref[...]
        wdt = y.shape[1]
        acc = acc + _dot(y, w_ref[c:c + wdt, :])
        c += wdt
    o_ref[...] = acc


def _outproj(x, ys, w_out, l):
    n, d = x.shape
    tm = min(PROJ_TM, n)
    return pl.pallas_call(
        _outproj_kernel,
        grid=(n // tm,),
        in_specs=[pl.BlockSpec((tm, d), lambda i: (i, 0))]
        + [pl.BlockSpec((tm, y.shape[1]), lambda i: (i, 0)) if y.ndim == 2
           else pl.BlockSpec((y.shape[0], tm, y.shape[2]), lambda i: (0, i, 0)) for y in ys]
        + [pl.BlockSpec((None, w_out.shape[1], d), lambda i: (l, 0, 0), pipeline_mode=pl.Buffered(1))],
        out_specs=pl.BlockSpec((tm, d), lambda i: (i, 0)),
        out_shape=jax.ShapeDtypeStruct((n, d), F32),
        compiler_params=_params("parallel"),
        name="out_proj",
    )(x, *ys, w_out)


def _rope_tables(seq):
    pos = jnp.arange(seq, dtype=F32)
    inv_freq = ROPE_THETA ** (-jnp.arange(0, ROPE_DIMS, 2, dtype=F32) / ROPE_DIMS)
    ang = pos[:, None] * inv_freq[None, :]
    cos, sin = jnp.cos(ang), jnp.sin(ang)
    rest = HEAD_DIM - ROPE_DIMS
    cos_t = jnp.concatenate([cos, cos, jnp.ones((seq, rest), F32)], axis=1)
    sin_t = jnp.concatenate([-sin, sin, jnp.zeros((seq, rest), F32)], axis=1)
    return cos_t, sin_t


def kernel(x, ffn_norm, ffn_w_gate, ffn_w_up, ffn_w_down, mix_norm, w_in, sg_norm, sg_w, sg_b, q_norm, k_norm, gla_w_gate, gla_b_gate, gla_out_norm, w_out):
    batch, seq, d = x.shape
    depth = w_in.shape[0]
    n = batch * seq
    sg_width = sg_norm.shape[-1]
    gla_qk = gla_w_gate.shape[-1]
    gla_width = (gla_qk // GLA_DK) * HEAD_DIM
    att_width = d - sg_width - gla_width
    n_in = w_in.shape[-1]
    assert n_in == 2 * sg_width + 3 * att_width + 2 * gla_qk + 2 * gla_width + GLA_GATE_RANK
    widths = (sg_width, 3 * att_width, 2 * gla_qk + 2 * gla_width + LANES)

    ffn_w = (ffn_w_gate, ffn_w_up, ffn_w_down)
    ffn_b = tuple(w[0, 0].astype(BF16) for w in ffn_w)
    w_in_b = w_in.astype(BF16)
    w_out_b = w_out.astype(BF16)
    gla_wg = jnp.pad(gla_w_gate, ((0, 0), (0, LANES - GLA_GATE_RANK), (0, 0))).astype(BF16)
    ffn_g = ffn_norm.reshape(depth, 2, 1, d)
    mix_g = mix_norm.reshape(depth, 1, d)
    sg_g = sg_norm.reshape(depth, 1, sg_width)
    sg_bt = jnp.swapaxes(sg_b, 1, 2)
    q_g = q_norm.reshape(depth, 1, HEAD_DIM)
    k_g = k_norm.reshape(depth, 1, HEAD_DIM)
    gla_bg = gla_b_gate.reshape(depth, 1, gla_qk)
    gla_og = gla_out_norm.reshape(depth, 1, HEAD_DIM)
    cos_t, sin_t = _rope_tables(seq)

    h = x.reshape(n, d)
    for l in range(depth):
        h, ffn_b = _ffn(h, ffn_g, ffn_b, ffn_w, (l, 1), l, 0)
        ya, za, zg, zd = _inproj(h, mix_g, w_in_b, cos_t, sin_t, q_g, k_g, sg_g, sg_w, sg_bt, l, widths)
        yb = _attention(za, zd)
        yc = _gla(zg, gla_wg, gla_bg, gla_og, l, batch)
        h = _outproj(h, (ya, yb, yc), w_out_b, l)
        h, ffn_b = _ffn(h, ffn_g, ffn_b, ffn_w, (l + 1, 0) if l + 1 < depth else None, l, 1)
    return h.reshape(batch, seq, d)
```

```python
import functools

import jax
import jax.numpy as jnp
from jax import lax
from jax.experimental import pallas as pl
from jax.experimental.pallas import tpu as pltpu

F32 = jnp.float32
BF16 = jnp.bfloat16

EPS = 1e-6
HEAD_DIM = 128
LANES = 128
SG_CHUNK = 128
ATT_DILATIONS = (1, 4, 16)
ATT_STEPS = 128
ATT_RESIDUES = 16
ROPE_THETA = 500000.0
ROPE_DIMS = HEAD_DIM // 4
GLA_DK = 64
GLA_CHUNK = 64
GLA_GATE_RANK = 16
GLA_GATE_NORMALIZER = 16.0
MASK_VALUE = -1e30

VMEM_LIMIT_BYTES = 58 * 1024 * 1024

FFN_TM, FFN_TF = 1024, 512
FFN_NORM_CHUNKS = 4
PROJ_TM = 512
OUT_TM = 1024
GLA_ROWS = 4096
ATT_GROUP = 8
LOG2_E = 1.4426950408889634


def _params(*semantics):
    return pltpu.CompilerParams(dimension_semantics=semantics, vmem_limit_bytes=VMEM_LIMIT_BYTES)


def _rms_norm(x, g):
    return x * lax.rsqrt(jnp.mean(x * x, axis=-1, keepdims=True) + EPS) * g


def _dot(a, b):
    return jnp.dot(a, b, preferred_element_type=F32)


def _dot_nt(a, b):
    return lax.dot_general(a, b, (((1,), (1,)), ((), ())), preferred_element_type=F32)


def _dot_tn(a, b):
    return lax.dot_general(a, b, (((0,), (0,)), ((), ())), preferred_element_type=F32)


def _ffn_kernel(x_ref, g_ref, wg_ref, wu_ref, wd_ref, *rest):
    n_next = (len(rest) - 2) // 2
    next_f32, o_ref, next_bf16, h_ref = rest[:n_next], rest[n_next], rest[n_next + 1:-1], rest[-1]

    def cast_next():
        for src, dst in zip(next_f32, next_bf16):
            dst[...] = src[...].astype(BF16)

    first = pl.program_id(1) == 0

    def half_swiglu(h):
        gate = _dot(h, wg_ref[...])
        up = _dot(h, wu_ref[...])
        return ((gate * jax.nn.sigmoid(gate)) * (0.5 * up)).astype(BF16)

    @pl.when(first)
    def _():
        cast_next()
        rows = x_ref.shape[0] // FFN_NORM_CHUNKS
        acts = []
        for c in range(FFN_NORM_CHUNKS):
            rs = slice(c * rows, (c + 1) * rows)
            h = _rms_norm(x_ref[rs, :], g_ref[...]).astype(BF16)
            h_ref[rs, :] = h
            acts.append(half_swiglu(h))
        for c, act in enumerate(acts):
            rs = slice(c * rows, (c + 1) * rows)
            o_ref[rs, :] = x_ref[rs, :] + _dot(act, wd_ref[...])

    @pl.when(jnp.logical_not(first))
    def _():
        cast_next()
        o_ref[...] += _dot(half_swiglu(h_ref[...]), wd_ref[...])


def _ffn(x, norm_g, weights, stacked_f32, nxt, l, j):
    n, d = x.shape
    w_gate, w_up, w_down = weights
    f = w_gate.shape[-1]
    tm, tf = min(FFN_TM, n), FFN_TF
    grid = (n // tm, f // tf)
    src, cast = (), []
    if nxt:
        src = tuple(stacked_f32)
        up_rows, down_rows = d // grid[0], f // (grid[0] * grid[1])
        assert up_rows * grid[0] == d and down_rows * grid[0] * grid[1] == f
        assert up_rows % 16 == 0 and down_rows % 16 == 0
        tile = ((up_rows, tf), lambda i, k: (i, k))
        cast = [tile, tile, ((down_rows, d), lambda i, k: (i * grid[1] + k, 0))]
    src_specs = [pl.BlockSpec((None, None) + b, lambda i, k, m=m: nxt + m(i, k)) for b, m in cast]
    dst_specs = [pl.BlockSpec(b, m) for b, m in cast]
    out = pl.pallas_call(
        _ffn_kernel,
        grid=grid,
        in_specs=[
            pl.BlockSpec((tm, d), lambda i, k: (i, 0)),
            pl.BlockSpec((None, None, 1, d), lambda i, k: (l, j, 0, 0)),
            pl.BlockSpec((d, tf), lambda i, k: (0, k)),
            pl.BlockSpec((d, tf), lambda i, k: (0, k)),
            pl.BlockSpec((tf, d), lambda i, k: (k, 0)),
        ] + src_specs,
        out_specs=[pl.BlockSpec((tm, d), lambda i, k: (i, 0))] + dst_specs,
        out_shape=[jax.ShapeDtypeStruct((n, d), F32)] + [jax.ShapeDtypeStruct(w.shape[2:], BF16) for w in src],
        scratch_shapes=[pltpu.VMEM((tm, d), BF16)],
        compiler_params=_params("parallel", "arbitrary"),
        name="ffn",
    )(x, norm_g, w_gate, w_up, w_down, *src)
    return out[0], tuple(out[1:])


def _inproj_kernel(x_ref, g_ref, w_ref, cos_ref, sin_ref, qg_ref, kg_ref, sgn_ref, sgw_ref, sgbt_ref,
                   ya_ref, za_ref, zg_ref, zd_ref, stage):
    h = _rms_norm(x_ref[...], g_ref[...]).astype(BF16)
    rows, sgw = ya_ref.shape
    c0 = 2 * sgw
    c1 = c0 + za_ref.shape[1]
    za = _dot(h, w_ref[:, c0:c1])
    zs = _dot(h, w_ref[:, :c0])
    c2 = c1 + zg_ref.shape[1] - LANES
    zg_ref[:, :c2 - c1] = _dot(h, w_ref[:, c1:c2]).astype(BF16)
    zg_ref[:, c2 - c1:] = jnp.zeros((rows, LANES), BF16)
    zg_ref[:, c2 - c1:c2 - c1 + GLA_GATE_RANK] = _dot(h, w_ref[:, c2:]).astype(BF16)

    heads = za_ref.shape[1] // (3 * HEAD_DIM)
    lane = lax.broadcasted_iota(jnp.int32, (1, HEAD_DIM), 1)
    half = ROPE_DIMS // 2
    cos = cos_ref[...]
    sin = sin_ref[...]
    res = zd_ref.shape[0]
    per = rows // res
    pitch = stage.shape[1] // res
    for j in range(3 * heads):
        cols = slice(j * HEAD_DIM, (j + 1) * HEAD_DIM)
        t = za[:, cols]
        if j < 2 * heads:
            gain, scale = (qg_ref, HEAD_DIM ** -0.5 * LOG2_E) if j < heads else (kg_ref, None)
            t = _rms_norm(t, gain[...])
            partner = jnp.where(lane < half, pltpu.roll(t, HEAD_DIM - half, 1), pltpu.roll(t, half, 1))
            t = t * cos + partner * sin
            if scale is not None:
                t = t * scale
        za_ref[:, cols] = t.astype(BF16)
        buf = stage.at[j % stage.shape[0]]
        for i in range(per):
            buf[pl.ds(i, res, stride=pitch), :] = t[i * res:(i + 1) * res, :]
        for r in range(res):
            zd_ref[r, :, cols] = buf[r * pitch:r * pitch + per, :].astype(BF16)

    t = SG_CHUNK
    chunks = rows // t
    causal = lax.broadcasted_iota(jnp.int32, (t, t), 1) <= lax.broadcasted_iota(jnp.int32, (t, t), 0)
    for g in range(sgw // HEAD_DIM):
        cu = slice(g * HEAD_DIM, (g + 1) * HEAD_DIM)
        cv = slice(sgw + g * HEAD_DIM, sgw + (g + 1) * HEAD_DIM)
        v = _rms_norm(jax.nn.gelu(zs[:, cv]), sgn_ref[:, cu]).astype(BF16)
        v = jnp.concatenate([v[c * t:(c + 1) * t] for c in range(chunks)], axis=1)
        sv = _dot(jnp.where(causal, sgw_ref[g], 0.0).astype(BF16), v) + sgbt_ref[:, g:g + 1]
        sv = jnp.concatenate([sv[:, c * HEAD_DIM:(c + 1) * HEAD_DIM] for c in range(chunks)], axis=0)
        ya_ref[:, cu] = (jax.nn.gelu(zs[:, cu]) * sv).astype(BF16)


def _inproj(x, norm_g, w_in, cos_t, sin_t, q_gain, k_gain, sg_norm, sg_w, sg_bt, l, widths):
    n, d = x.shape
    c = w_in.shape[-1]
    seq = cos_t.shape[0]
    batch = n // seq
    sgw = sg_norm.shape[-1]
    groups = sgw // HEAD_DIM
    res = ATT_RESIDUES
    tm = min(PROJ_TM, seq)
    tiles = seq // tm
    per = tm // res
    assert seq % tm == 0 and tm % SG_CHUNK == 0 and per % 16 == 0
    pitch = per + 8

    def layer(*block):
        return pl.BlockSpec((None,) + block, lambda i: (l,) + (0,) * len(block))

    return pl.pallas_call(
        _inproj_kernel,
        grid=(n // tm,),
        in_specs=[
            pl.BlockSpec((tm, d), lambda i: (i, 0)),
            layer(1, d),
            pl.BlockSpec((None, d, c), lambda i: (l, 0, 0), pipeline_mode=pl.Buffered(1)),
            pl.BlockSpec((tm, HEAD_DIM), lambda i: (i % tiles, 0)),
            pl.BlockSpec((tm, HEAD_DIM), lambda i: (i % tiles, 0)),
            layer(1, HEAD_DIM),
            layer(1, HEAD_DIM),
            layer(1, sgw),
            layer(groups, SG_CHUNK, SG_CHUNK),
            layer(SG_CHUNK, groups),
        ],
        out_specs=[pl.BlockSpec((tm, w), lambda i: (i, 0)) for w in widths]
        + [pl.BlockSpec((None, res, per, widths[1]), lambda i: (i // tiles, 0, i % tiles, 0))],
        out_shape=[jax.ShapeDtypeStruct((n, w), BF16) for w in widths]
        + [jax.ShapeDtypeStruct((batch, res, seq // res, widths[1]), BF16)],
        scratch_shapes=[pltpu.VMEM((2, res * pitch, HEAD_DIM), F32)],
        compiler_params=_params("parallel"),
        name="in_proj",
    )(x, norm_g, w_in, cos_t, sin_t, q_gain, k_gain, sg_norm, sg_w, sg_bt)


def _att_kernel(qn, kn, v_ref, qd, kd, vd, o_ref, acc, m_s, l_s, s_buf, mb_buf):
    seq = qn.shape[0]
    w = ATT_STEPS
    res = ATT_RESIDUES
    per = seq // res
    pitch = acc.shape[0] // res

    row = lax.broadcasted_iota(jnp.int32, (w, w), 0)
    col = lax.broadcasted_iota(jnp.int32, (w, w), 1)
    group = s_buf.shape[0]

    def run_branch(load_qk, load_v, load_state, store, order_diff):
        def scores(g):
            for u in range(group):
                q, k2, has_prev = load_qk(g * group + u)
                s = _dot_nt(q, k2)
                prev_min = jnp.where(has_prev, 0, 2 * w)
                s = jnp.concatenate([jnp.where(order_diff >= prev_min, s[:, :w], MASK_VALUE),
                                     jnp.where(order_diff <= 0, s[:, w:], MASK_VALUE)], axis=1)
                s_buf[u] = s
                mb_buf[u] = jnp.broadcast_to(jnp.max(s, axis=1, keepdims=True), (w, HEAD_DIM))

        def consume(g):
            for u in range(group):
                i = g * group + u
                old = load_state(i)
                m_blk = mb_buf[u]
                m_new = m_blk if old is None else jnp.maximum(old[0], m_blk)
                p = jnp.exp2(s_buf[u] - jnp.concatenate([m_new, m_new], axis=1))
                p_sum = jnp.sum(p, axis=1, keepdims=True)
                pv = _dot(p.astype(BF16), load_v(i))
                if old is None:
                    store(i, m_new, jnp.broadcast_to(p_sum, (w, HEAD_DIM)), pv)
                else:
                    alpha = jnp.exp2(old[0] - m_new)
                    store(i, m_new, alpha * old[1] + p_sum, alpha * old[2] + pv)

        def body(g, carry):
            consume(g - 1)
            scores(g)
            return carry

        groups = seq // (w * group)
        scores(0)
        lax.fori_loop(1, groups, body, 0)
        consume(groups - 1)

    for d in ATT_DILATIONS[:0:-1]:
        first = d == ATT_DILATIONS[-1]
        pieces = res // d
        plen = w // pieces
        nb = per // plen
        order = lambda x: pieces * (x % plen) + x // plen

        def gather(ref, r, at, d=d, pieces=pieces, plen=plen):
            return jnp.concatenate([ref[c * d + r, pl.ds(at, plen), :] for c in range(pieces)], axis=0)

        def starts(i, nb=nb, plen=plen):
            n = i % nb
            return (i // nb, n, pl.multiple_of(n * plen, plen),
                    pl.multiple_of(jnp.where(n > 0, n - 1, n) * plen, plen))

        def state_rows(r, c, start, d=d, plen=plen):
            return pl.ds(pl.multiple_of((c * d + r) * pitch + start, 8), plen)

        def load_qk(i, gather=gather, starts=starts):
            r, n, start, pstart = starts(i)
            return gather(qd, r, start), jnp.concatenate([gather(kd, r, pstart), gather(kd, r, start)], axis=0), n > 0

        def load_v(i, gather=gather, starts=starts):
            r, _, start, pstart = starts(i)
            return jnp.concatenate([gather(vd, r, pstart), gather(vd, r, start)], axis=0)

        def load_state(i, first=first, pieces=pieces, starts=starts, state_rows=state_rows):
            if first:
                return None
            r, _, start, _ = starts(i)
            return tuple(jnp.concatenate([ref[state_rows(r, c, start), :] for c in range(pieces)], axis=0)
                         for ref in (m_s, l_s, acc))

        def store(i, *new, pieces=pieces, plen=plen, starts=starts, state_rows=state_rows):
            r, _, start, _ = starts(i)
            for ref, val in zip((m_s, l_s, acc), new):
                for c in range(pieces):
                    ref[state_rows(r, c, start), :] = val[c * plen:(c + 1) * plen, :]

        run_branch(load_qk, load_v, load_state, store, order(col) - order(row))

    def nat_starts(n):
        return pl.multiple_of(n * w, w), pl.multiple_of(jnp.where(n > 0, n - 1, n) * w, w)

    def nat_qk(n):
        start, pstart = nat_starts(n)
        return (qn[pl.ds(start, w), :],
                jnp.concatenate([kn[pl.ds(pstart, w), :], kn[pl.ds(start, w), :]], axis=0), n > 0)

    def nat_v(n):
        start, pstart = nat_starts(n)
        return jnp.concatenate([v_ref[pl.ds(pstart, w), :], v_ref[pl.ds(start, w), :]], axis=0)

    def nat_state(n):
        return tuple(jnp.concatenate([ref[pl.ds(n * (w // res) + j, res, stride=pitch), :]
                                      for j in range(w // res)], axis=0) for ref in (m_s, l_s, acc))

    def emit(n, m_new, l_new, acc_new):
        o_ref[pl.ds(nat_starts(n)[0], w), :] = (acc_new / l_new).astype(BF16)

    run_branch(nat_qk, nat_v, nat_state, emit, col - row)


def _attention(za, zd):
    n, w3 = za.shape
    batch, res, per, _ = zd.shape
    seq = n // batch
    width = w3 // 3
    heads = width // HEAD_DIM
    assert res == ATT_RESIDUES and per * res == seq and seq % (ATT_STEPS * res) == 0
    pitch = per + 8
    za3 = za.reshape(batch, seq, w3)
    blk = (None, seq, HEAD_DIM)
    blk_d = (None, res, per, HEAD_DIM)
    out = pl.pallas_call(
        _att_kernel,
        grid=(batch, heads),
        in_specs=[
            pl.BlockSpec(blk, lambda b, h: (b, 0, h)),
            pl.BlockSpec(blk, lambda b, h: (b, 0, heads + h)),
            pl.BlockSpec(blk, lambda b, h: (b, 0, 2 * heads + h)),
            pl.BlockSpec(blk_d, lambda b, h: (b, 0, 0, h)),
            pl.BlockSpec(blk_d, lambda b, h: (b, 0, 0, heads + h)),
            pl.BlockSpec(blk_d, lambda b, h: (b, 0, 0, 2 * heads + h)),
        ],
        out_specs=pl.BlockSpec(blk, lambda b, h: (b, 0, h)),
        out_shape=jax.ShapeDtypeStruct((batch, seq, width), BF16),
        scratch_shapes=[pltpu.VMEM((res * pitch, HEAD_DIM), F32) for _ in range(3)]
        + [pltpu.VMEM((ATT_GROUP, ATT_STEPS, 2 * ATT_STEPS), F32), pltpu.VMEM((ATT_GROUP, ATT_STEPS, HEAD_DIM), F32)],
        compiler_params=_params("parallel", "parallel"),
        name="dilated_attention",
    )(za3, za3, za3, zd, zd, zd)
    return out.reshape(n, width)


def _gla_kernel(q_ref, k_ref, v_ref, g_ref, r_ref, wg_ref, bg_ref, og_ref, o_ref, st_ref, o_acc):
    rows = q_ref.shape[0]
    ck = GLA_CHUNK
    dv = HEAD_DIM

    @pl.when(pl.program_id(2) == 0)
    def _():
        st_ref[...] = jnp.zeros_like(st_ref)

    x = _dot(r_ref[...], wg_ref[...]) + bg_ref[...]
    log_a = (jnp.minimum(x, 0.0) - jnp.log(1.0 + jnp.exp(-jnp.abs(x)))) / GLA_GATE_NORMALIZER
    hi = log_a.astype(BF16)
    lo = (log_a - hi.astype(F32)).astype(BF16)
    grp = 4 * ck
    r_i = lax.broadcasted_iota(jnp.int32, (grp, grp), 0)
    c_i = lax.broadcasted_iota(jnp.int32, (grp, grp), 1)
    tri = jnp.where(((r_i // ck) == (c_i // ck)) & (c_i <= r_i), 1.0, 0.0).astype(BF16)
    b = jnp.concatenate([_dot(tri, hi[s * grp:(s + 1) * grp]) + _dot(tri, lo[s * grp:(s + 1) * grp])
                         for s in range(rows // grp)], axis=0)

    q = q_ref[...].astype(F32) * GLA_DK ** -0.5
    k = k_ref[...].astype(F32)
    q_dec = (q * jnp.exp(b)).astype(BF16)
    k_inv = (k * jnp.exp(-b)).astype(BF16)

    lane = lax.broadcasted_iota(jnp.int32, (1, 2 * GLA_DK), 1)
    head0 = lane < GLA_DK
    ar = lax.broadcasted_iota(jnp.int32, (ck, 2 * ck), 0)
    ac = lax.broadcasted_iota(jnp.int32, (ck, 2 * ck), 1)
    causal = (ac % ck) <= ar
    sr = lax.broadcasted_iota(jnp.int32, (2 * dv, 2 * GLA_DK), 0)
    sc = lax.broadcasted_iota(jnp.int32, (2 * dv, 2 * GLA_DK), 1)
    own = (sr < dv) == (sc < GLA_DK)
    zeros_v = jnp.zeros((ck, dv), BF16)

    chunks = [slice(c * ck, (c + 1) * ck) for c in range(rows // ck)]
    scores, incs, decays = [], [], []
    for rs in chunks:
        ki = k_inv[rs]
        ki2 = jnp.concatenate([jnp.where(head0, ki, 0), jnp.where(head0, 0, ki)], axis=0)
        scores.append(_dot_nt(q_dec[rs], ki2))
    for rs in chunks:
        b_last = b[rs.stop - 1:rs.stop, :]
        k_dec = (k[rs] * jnp.exp(b_last - b[rs])).astype(BF16)
        incs.append(_dot_tn(v_ref[rs, :], k_dec))
        decays.append(jnp.exp(b_last))
    for rs, a in zip(chunks, scores):
        vp = v_ref[rs, :]
        v_bd = jnp.concatenate(
            [jnp.concatenate([vp[:, :dv], zeros_v], axis=1),
             jnp.concatenate([zeros_v, vp[:, dv:]], axis=1)], axis=0)
        o_acc[rs, :] = _dot(jnp.where(causal, a, 0.0).astype(BF16), v_bd)
    state = st_ref[...]
    for rs, inc, decay in zip(chunks, incs, decays):
        o_acc[rs, :] += _dot_nt(q_dec[rs], state.astype(BF16))
        state = state * decay + jnp.where(own, inc, 0.0)
    st_ref[...] = state

    for h in range(2):
        cs = slice(h * dv, (h + 1) * dv)
        g = g_ref[:, cs].astype(F32)
        o = _rms_norm(o_acc[:, cs], og_ref[...]) * (g * jax.nn.sigmoid(g))
        o_ref[:, cs] = o.astype(BF16)


def _gla(zg, w_gate, b_gate, out_gain, l, batch):
    n, _ = zg.shape
    seq = n // batch
    dkp = 2 * GLA_DK
    dvp = 2 * HEAD_DIM
    pairs = w_gate.shape[-1] // dkp
    width = pairs * dvp
    rows = min(GLA_ROWS, seq)
    zg3 = zg.reshape(batch, seq, zg.shape[1])
    qk_blocks = 2 * pairs * dkp // dvp
    out = pl.pallas_call(
        _gla_kernel,
        grid=(batch, pairs, seq // rows),
        in_specs=[
            pl.BlockSpec((None, rows, dkp), lambda b, p, t: (b, t, p)),
            pl.BlockSpec((None, rows, dkp), lambda b, p, t: (b, t, pairs + p)),
            pl.BlockSpec((None, rows, dvp), lambda b, p, t: (b, t, qk_blocks + p)),
            pl.BlockSpec((None, rows, dvp), lambda b, p, t: (b, t, qk_blocks + pairs + p)),
            pl.BlockSpec((None, rows, LANES), lambda b, p, t: (b, t, (2 * pairs * dkp + 2 * width) // LANES)),
            pl.BlockSpec((None, LANES, dkp), lambda b, p, t: (l, 0, p)),
            pl.BlockSpec((None, 1, dkp), lambda b, p, t: (l, 0, p)),
            pl.BlockSpec((None, 1, HEAD_DIM), lambda b, p, t: (l, 0, 0)),
        ],
        out_specs=pl.BlockSpec((None, rows, dvp), lambda b, p, t: (b, t, p)),
        out_shape=jax.ShapeDtypeStruct((batch, seq, width), BF16),
        scratch_shapes=[pltpu.VMEM((dvp, dkp), F32), pltpu.VMEM((rows, dvp), F32)],
        compiler_params=_params("parallel", "parallel", "arbitrary"),
        name="gated_linear_attention",
    )(zg3, zg3, zg3, zg3, zg3, w_gate, b_gate, out_gain)
    return out.reshape(n, width)


def _outproj_kernel(x_ref, ya_ref, yb_ref, yc_ref, w_ref, o_ref):
    acc = x_ref[...]
    c = 0
    for y_ref in (ya_ref, yb_ref, yc_ref):
        wdt = y_ref.shape[1]
        acc = acc + _dot(y_ref[...], w_ref[c:c + wdt, :])
        c += wdt
    o_ref[...] = acc


def _outproj(x, ys, w_out, l):
    n, d = x.shape
    tm = min(OUT_TM, n)
    return pl.pallas_call(
        _outproj_kernel,
        grid=(n // tm,),
        in_specs=[pl.BlockSpec((tm, d), lambda i: (i, 0))]
        + [pl.BlockSpec((tm, y.shape[1]), lambda i: (i, 0)) for y in ys]
        + [pl.BlockSpec((None, w_out.shape[1], d), lambda i: (l, 0, 0), pipeline_mode=pl.Buffered(1))],
        out_specs=pl.BlockSpec((tm, d), lambda i: (i, 0)),
        out_shape=jax.ShapeDtypeStruct((n, d), F32),
        compiler_params=_params("parallel"),
        name="out_proj",
    )(x, *ys, w_out)


def _rope_tables(seq):
    pos = jnp.arange(seq, dtype=F32)
    inv_freq = ROPE_THETA ** (-jnp.arange(0, ROPE_DIMS, 2, dtype=F32) / ROPE_DIMS)
    ang = pos[:, None] * inv_freq[None, :]
    cos, sin = jnp.cos(ang), jnp.sin(ang)
    rest = HEAD_DIM - ROPE_DIMS
    cos_t = jnp.concatenate([cos, cos, jnp.ones((seq, rest), F32)], axis=1)
    sin_t = jnp.concatenate([-sin, sin, jnp.zeros((seq, rest), F32)], axis=1)
    return cos_t, sin_t


def kernel(x, ffn_norm, ffn_w_gate, ffn_w_up, ffn_w_down, mix_norm, w_in, sg_norm, sg_w, sg_b, q_norm, k_norm, gla_w_gate, gla_b_gate, gla_out_norm, w_out):
    batch, seq, d = x.shape
    depth = w_in.shape[0]
    n = batch * seq
    sg_width = sg_norm.shape[-1]
    gla_qk = gla_w_gate.shape[-1]
    gla_width = (gla_qk // GLA_DK) * HEAD_DIM
    att_width = d - sg_width - gla_width
    n_in = w_in.shape[-1]
    assert n_in == 2 * sg_width + 3 * att_width + 2 * gla_qk + 2 * gla_width + GLA_GATE_RANK
    widths = (sg_width, 3 * att_width, 2 * gla_qk + 2 * gla_width + LANES)

    ffn_w = (ffn_w_gate, ffn_w_up, ffn_w_down)
    ffn_b = tuple(w[0, 0].astype(BF16) for w in ffn_w)
    w_in_b = w_in.astype(BF16)
    w_out_b = w_out.astype(BF16)
    gla_wg = jnp.pad(gla_w_gate, ((0, 0), (0, LANES - GLA_GATE_RANK), (0, 0))).astype(BF16)
    ffn_g = ffn_norm.reshape(depth, 2, 1, d)
    mix_g = mix_norm.reshape(depth, 1, d)
    sg_g = sg_norm.reshape(depth, 1, sg_width)
    sg_bt = jnp.swapaxes(sg_b, 1, 2)
    q_g = q_norm.reshape(depth, 1, HEAD_DIM)
    k_g = k_norm.reshape(depth, 1, HEAD_DIM)
    gla_bg = gla_b_gate.reshape(depth, 1, gla_qk)
    gla_og = gla_out_norm.reshape(depth, 1, HEAD_DIM)
    cos_t, sin_t = _rope_tables(seq)

    h = x.reshape(n, d)
    for l in range(depth):
        h, ffn_b = _ffn(h, ffn_g, ffn_b, ffn_w, (l, 1), l, 0)
        ya, za, zg, zd = _inproj(h, mix_g, w_in_b, cos_t, sin_t, q_g, k_g, sg_g, sg_w, sg_bt, l, widths)
        yb = _attention(za, zd)
        yc = _gla(zg, gla_wg, gla_bg, gla_og, l, batch)
        h = _outproj(h, (ya, yb, yc), w_out_b, l)
        h, ffn_b = _ffn(h, ffn_g, ffn_b, ffn_w, (l + 1, 0) if l + 1 < depth else None, l, 1)
    return h.reshape(batch, seq, d)
```
